```python
import jax, jax.numpy as jnp
from jax import lax
import numpy as np

D_MODEL = 1024
BATCH = 32
SEQ = 256
DEPTH = 1
DEC_BATCH = 4
DEC_SEQ = 4096
PAST_LEN = 256

GRID_W = 64
H_A = 8
N_A = 64
D_A = H_A * N_A
H_B = 8
N_B = 64
D_B = H_B * N_B
LORA_W = 64
LORA_A = 64
LORA_G = 128
D_A_COLS = 3 * D_A + LORA_W + LORA_A + LORA_G
D_IN = D_A_COLS + 3 * D_B
WIN_R = 8
WIN_C = 16
N_EXPERTS = 16
EC_FACTOR = 2
D_EXPERT = 2816
RMS_EPS = 1e-6
GN_EPS = 64e-5
ATTN_SCALE = N_B ** -0.5
NEG_INF = -1e30

kernel_name = 'hybrid_rwkv7_natten_ec_diffusion_step'


def rmsnorm(x, g):
    xf = x.astype(jnp.float32)
    y = xf * lax.rsqrt(jnp.mean(xf * xf, axis=-1, keepdims=True) + RMS_EPS)
    return y.astype(x.dtype) * g


def adaln(cond, w_mod, b_mod):
    mod = jax.nn.silu(cond) @ w_mod + b_mod
    return tuple(m[:, None, :] for m in jnp.split(mod, 6, axis=-1))


def centred_shift(p, mu_prev, mu_next):
    zero = jnp.zeros_like(p[:, :1])
    p_prev = jnp.concatenate([zero, p[:, :-1]], axis=1)
    p_next = jnp.concatenate([p[:, 1:], zero], axis=1)
    return p + mu_prev * (p_prev - p) + mu_next * (p_next - p)


def rwkv_scan(r, decay, k, v, kk, a, s0, reverse):
    xs = tuple(jnp.moveaxis(t.astype(jnp.float32), 1, 0) for t in (r, decay, k, v, kk, a))

    def step(s, inp):
        r_t, w_t, k_t, v_t, kk_t, a_t = inp
        sa = jnp.einsum('bhvk,bhk->bhv', s, -kk_t)
        s = (s * w_t[:, :, None, :] + sa[..., None] * (kk_t * a_t)[:, :, None, :]
             + v_t[..., None] * k_t[:, :, None, :])
        return s, jnp.einsum('bhvk,bhk->bhv', s, r_t)

    s_fin, ys = lax.scan(step, s0.astype(jnp.float32), xs, reverse=reverse)
    return jnp.moveaxis(ys, 0, 1), s_fin


def rwkv_mixer(pa, s0, w0, w_up, a0, a_up, g_up, k_k, k_a, r_k, gn_g, gn_b):
    b, t, _ = pa.shape
    idx = [D_A, 2 * D_A, 3 * D_A, 3 * D_A + LORA_W, 3 * D_A + LORA_W + LORA_A]
    r, k, v, wlo, alo, glo = jnp.split(pa, idx, axis=-1)
    heads = lambda z: z.reshape(b, t, H_A, N_A)
    kkf = heads((k * k_k).astype(jnp.float32))
    kk = kkf * lax.rsqrt(jnp.sum(kkf * kkf, axis=-1, keepdims=True) + 1e-12)
    g = jax.nn.sigmoid(glo) @ g_up
    ys, bonuses, finals = [], [], []
    for d, rev in enumerate((False, True)):
        w_log = -jax.nn.softplus(-(w0[d] + jnp.tanh(wlo) @ w_up[d]).astype(jnp.float32)) - 0.5
        decay = jnp.exp(-jnp.exp(w_log))
        a = jax.nn.sigmoid(a0[d] + alo @ a_up[d])
        kd = k * (1 + (a - 1) * k_a)
        y, s_fin = rwkv_scan(heads(r), heads(decay), heads(kd), heads(v), kk, heads(a), s0[:, d], rev)
        ys.append(y)
        bonuses.append(jnp.sum((heads(r) * heads(kd) * r_k).astype(jnp.float32), axis=-1, keepdims=True))
        finals.append(s_fin)
    yf = ys[0] + ys[1]
    mu = jnp.mean(yf, axis=-1, keepdims=True)
    var = jnp.mean(jnp.square(yf - mu), axis=-1, keepdims=True)
    yn = ((yf - mu) * lax.rsqrt(var + GN_EPS)).reshape(b, t, D_A).astype(pa.dtype)
    y = yn * gn_g + gn_b
    y = y + ((bonuses[0] + bonuses[1]).astype(pa.dtype) * heads(v)).reshape(b, t, D_A)
    return y * g, jnp.stack(finals, axis=1).astype(pa.dtype)


def context_attention(q, k, v):
    b, t, _, _ = q.shape
    s = jnp.einsum('bqhn,bkhn->bhqk', q, k).astype(jnp.float32) * ATTN_SCALE
    p = jax.nn.softmax(s, axis=-1).astype(v.dtype)
    return jnp.einsum('bhqk,bkhn->bqhn', p, v).reshape(b, t, D_B)


def neighbourhood_attention(q, k, v, ctx_k, ctx_v, rpb):
    b, t, h, n = q.shape
    rows = t // GRID_W
    wr = min(WIN_R, rows)
    r_ids = jnp.arange(rows)
    row_start = jnp.clip(r_ids - wr // 2, 0, rows - wr)
    key_rows = row_start[:, None] + jnp.arange(wr)[None, :]
    kg = k.reshape(b, rows, GRID_W, h, n)[:, key_rows].reshape(b, rows, wr * GRID_W, h, n)
    vg = v.reshape(b, rows, GRID_W, h, n)[:, key_rows].reshape(b, rows, wr * GRID_W, h, n)
    qr = q.reshape(b, rows, GRID_W, h, n)
    s_loc = jnp.einsum('brqhn,brkhn->bhrqk', qr, kg).astype(jnp.float32) * ATTN_SCALE
    c_ids = jnp.arange(GRID_W)
    col_start = jnp.clip(c_ids - WIN_C // 2, 0, GRID_W - WIN_C)
    key_col = jnp.arange(wr * GRID_W) % GRID_W
    mask = (key_col[None, :] >= col_start[:, None]) & (key_col[None, :] < col_start[:, None] + WIN_C)
    dr_idx = key_rows - r_ids[:, None] + WIN_R - 1
    dc_idx = jnp.clip(c_ids[None, :] - c_ids[:, None] + WIN_C - 1, 0, 2 * WIN_C - 2)
    bias = rpb[:, dr_idx[:, None, :, None], dc_idx[None, :, None, :]]
    bias = bias.reshape(h, rows, GRID_W, wr * GRID_W).astype(jnp.float32)
    s_loc = jnp.where(mask, s_loc + bias[None], NEG_INF)
    s_ctx = jnp.einsum('brqhn,blhn->bhrql', qr, ctx_k).astype(jnp.float32) * ATTN_SCALE
    p = jax.nn.softmax(jnp.concatenate([s_loc, s_ctx], axis=-1), axis=-1).astype(v.dtype)
    n_loc = wr * GRID_W
    o = (jnp.einsum('bhrqk,brkhn->brqhn', p[..., :n_loc], vg)
         + jnp.einsum('bhrql,blhn->brqhn', p[..., n_loc:], ctx_v))
    return o.reshape(b, t, D_B)


def expert_choice_ffn(h, router, e_gate, e_up, e_down):
    b, t, _ = h.shape
    cap = EC_FACTOR * t // N_EXPERTS
    aff = jax.nn.softmax(jnp.einsum('btd,de->bte', h, router).astype(jnp.float32), axis=-1)
    gate, idx = lax.top_k(jnp.swapaxes(aff, 1, 2), cap)
    bidx = jnp.arange(b)[:, None, None]
    xe = h[bidx, idx]
    hid = jax.nn.silu(jnp.einsum('becd,edf->becf', xe, e_gate)) * jnp.einsum('becd,edf->becf', xe, e_up)
    ye = jnp.einsum('becf,efd->becd', hid, e_down) * gate[..., None].astype(h.dtype)
    return jnp.zeros_like(h).at[bidx, idx].add(ye)


def trunk_layer(x, cond, s0, attend, norm1_g, norm2_g, w_mod, b_mod, w_in, mu_prev, mu_next,
                w0, w_up, a0, a_up, g_up, k_k, k_a, r_k, gn_g, gn_b, w_out,
                router, e_gate, e_up, e_down):
    b, t, _ = x.shape
    sh1, sc1, gt1, sh2, sc2, gt2 = adaln(cond, w_mod, b_mod)
    h = rmsnorm(x, norm1_g) * (1 + sc1) + sh1
    proj = jnp.einsum('btd,de->bte', h, w_in)
    pa = centred_shift(proj[..., :D_A_COLS], mu_prev, mu_next)
    qkv = proj[..., D_A_COLS:].reshape(b, t, 3, H_B, N_B)
    q, k, v = qkv[:, :, 0], qkv[:, :, 1], qkv[:, :, 2]
    ya, s_fin = rwkv_mixer(pa, s0, w0, w_up, a0, a_up, g_up, k_k, k_a, r_k, gn_g, gn_b)
    yb = attend(q, k, v)
    x = x + gt1 * jnp.einsum('bte,ed->btd', jnp.concatenate([ya, yb], axis=-1), w_out)
    h = rmsnorm(x, norm2_g) * (1 + sc2) + sh2
    x = x + gt2 * expert_choice_ffn(h, router, e_gate, e_up, e_down)
    return x, k, v, s_fin


def setup_inputs(seed: int = 0) -> dict:
    key = jax.random.key(seed)
    ks = jax.random.split(key, 40)
    nrm = lambda i, shape, s: jax.random.normal(ks[i], shape, jnp.float32) * s
    return {
        'x_prompt': nrm(0, (BATCH, SEQ, D_MODEL), 1.0),
        'x_sample': nrm(1, (DEC_BATCH, DEC_SEQ, D_MODEL), 1.0),
        'cache_na_k': nrm(2, (DEC_BATCH, DEPTH, PAST_LEN, H_B, N_B), 1.0),
        'cache_na_v': nrm(3, (DEC_BATCH, DEPTH, PAST_LEN, H_B, N_B), 1.0),
        'state_rwkv': nrm(4, (DEC_BATCH, DEPTH, 2, H_A, N_A, N_A), 0.5),
        'c': nrm(5, (DEC_BATCH, D_MODEL), 1.0),
        'c_ctx': nrm(6, (D_MODEL,), 1.0),
        'final_norm_g': 1.0 + nrm(7, (D_MODEL,), 0.05),
        'norm1_g': 1.0 + nrm(8, (DEPTH, D_MODEL), 0.05),
        'norm2_g': 1.0 + nrm(9, (DEPTH, D_MODEL), 0.05),
        'w_mod': nrm(10, (DEPTH, D_MODEL, 6 * D_MODEL), D_MODEL ** -0.5),
        'b_mod': nrm(11, (DEPTH, 6 * D_MODEL), 0.01),
        'w_in': nrm(12, (DEPTH, D_MODEL, D_IN), D_MODEL ** -0.5),
        'mu_prev': jax.random.uniform(ks[13], (DEPTH, D_A_COLS), jnp.float32, 0.0, 0.5),
        'mu_next': jax.random.uniform(ks[14], (DEPTH, D_A_COLS), jnp.float32, 0.0, 0.5),
        'w0': nrm(15, (DEPTH, 2, D_A), 0.5),
        'w_up': nrm(16, (DEPTH, 2, LORA_W, D_A), LORA_W ** -0.5),
        'a0': nrm(17, (DEPTH, 2, D_A), 0.5),
        'a_up': nrm(18, (DEPTH, 2, LORA_A, D_A), LORA_A ** -0.5),
        'g_up': nrm(19, (DEPTH, LORA_G, D_A), LORA_G ** -0.5),
        'k_k': 0.85 + nrm(20, (DEPTH, D_A), 0.05),
        'k_a': 1.0 + nrm(21, (DEPTH, D_A), 0.05),
        'r_k': nrm(22, (DEPTH, H_A, N_A), 0.1),
        'gn_g': 1.0 + nrm(23, (DEPTH, D_A), 0.05),
        'gn_b': nrm(24, (DEPTH, D_A), 0.01),
        'rpb': nrm(25, (DEPTH, H_B, 2 * WIN_R - 1, 2 * WIN_C - 1), 0.1),
        'w_out': nrm(26, (DEPTH, D_MODEL, D_MODEL), D_MODEL ** -0.5),
        'router': nrm(27, (DEPTH, D_MODEL, N_EXPERTS), D_MODEL ** -0.5),
        'e_gate': nrm(28, (DEPTH, N_EXPERTS, D_MODEL, D_EXPERT), D_MODEL ** -0.5),
        'e_up': nrm(29, (DEPTH, N_EXPERTS, D_MODEL, D_EXPERT), D_MODEL ** -0.5),
        'e_down': nrm(30, (DEPTH, N_EXPERTS, D_EXPERT, D_MODEL), D_EXPERT ** -0.5),
    }


def reference(x_prompt, x_sample, cache_na_k, cache_na_v, state_rwkv, c, c_ctx, final_norm_g,
              norm1_g, norm2_g, w_mod, b_mod, w_in, mu_prev, mu_next, w0, w_up, a0, a_up, g_up,
              k_k, k_a, r_k, gn_g, gn_b, rpb, w_out, router, e_gate, e_up, e_down):
    xp = x_prompt
    xs = x_sample
    new_k, new_v, new_s = [], [], []
    for l in range(DEPTH):
        lw = (norm1_g[l], norm2_g[l], w_mod[l], b_mod[l], w_in[l], mu_prev[l], mu_next[l],
              w0[l], w_up[l], a0[l], a_up[l], g_up[l], k_k[l], k_a[l], r_k[l], gn_g[l], gn_b[l],
              w_out[l], router[l], e_gate[l], e_up[l], e_down[l])
        s0_ctx = jnp.zeros((xp.shape[0], 2, H_A, N_A, N_A), jnp.float32)
        xp, k_ctx, v_ctx, s_ctx = trunk_layer(xp, c_ctx[None, :], s0_ctx, context_attention, *lw)
        new_k.append(k_ctx)
        new_v.append(v_ctx)
        new_s.append(s_ctx)
        ck, cv, rl = cache_na_k[:, l], cache_na_v[:, l], rpb[l]
        attend_latent = lambda q, k, v, ck=ck, cv=cv, rl=rl: neighbourhood_attention(q, k, v, ck, cv, rl)
        xs, _, _, _ = trunk_layer(xs, c, state_rwkv[:, l], attend_latent, *lw)
    y_prompt = rmsnorm(xp, final_norm_g)
    y_sample = rmsnorm(xs, final_norm_g)
    return (y_prompt, y_sample, jnp.stack(new_k, axis=1), jnp.stack(new_v, axis=1), jnp.stack(new_s, axis=1))
```

```python
import functools

import numpy as np
import jax
import jax.numpy as jnp
from jax import lax
from jax.experimental import pallas as pl
from jax.experimental.pallas import tpu as pltpu

D_MODEL = 1024
GRID_W = 64
H_A = 8
N_A = 64
D_A = H_A * N_A
H_B = 8
N_B = 64
D_B = H_B * N_B
LORA_W = 64
LORA_A = 64
LORA_G = 128
D_A_COLS = 3 * D_A + LORA_W + LORA_A + LORA_G
D_IN = D_A_COLS + 3 * D_B
WIN_R = 8
WIN_C = 16
N_EXPERTS = 16
EC_FACTOR = 2
D_EXPERT = 2816
RMS_EPS = 1e-6
GN_EPS = 64e-5
ATTN_SCALE = N_B ** -0.5
NEG_INF = -1e30

CHUNK = 64
HEAD_GROUP = 4
GROUP_LANES = HEAD_GROUP * N_A
VMEM_LIMIT = 56 * 1024 * 1024

_HI = lax.Precision.HIGHEST
_NN = (((1,), (0,)), ((), ()))
_NT = (((1,), (1,)), ((), ()))
_TN = (((0,), (0,)), ((), ()))


def _dot(a, b, dims=_NN, precision=_HI):
    return lax.dot_general(a, b, dims, precision=precision, preferred_element_type=jnp.float32)


def _mod_kernel(c_ref, w_ref, b_ref, o_ref):
    c = c_ref[...]
    s = c * jax.nn.sigmoid(c)
    o_ref[...] = _dot(s.astype(jnp.bfloat16), w_ref[...].astype(jnp.bfloat16), precision=None) + b_ref[...]


def _adaln(cond8, w_mod, b_mod):
    n = w_mod.shape[1]
    tn = 1024
    return pl.pallas_call(
        _mod_kernel,
        grid=(n // tn,),
        in_specs=[pl.BlockSpec((8, D_MODEL), lambda j: (0, 0)),
                  pl.BlockSpec((D_MODEL, tn), lambda j: (0, j)),
                  pl.BlockSpec((1, tn), lambda j: (0, j))],
        out_specs=pl.BlockSpec((8, tn), lambda j: (0, j)),
        out_shape=jax.ShapeDtypeStruct((8, n), jnp.float32),
        compiler_params=pltpu.CompilerParams(dimension_semantics=("arbitrary",),
                                             vmem_limit_bytes=VMEM_LIMIT),
        name="adaln_mod",
    )(cond8, w_mod, b_mod.reshape(1, n))


def _rms_mod(x, g, scale, shift):
    y = x * lax.rsqrt(jnp.mean(x * x, axis=-1, keepdims=True) + RMS_EPS)
    return (y * g) * (1.0 + scale) + shift


def _inproj_kernel(x_ref, mod_ref, g_ref, w_ref, pa_ref, qkv_ref):
    mod = mod_ref[0]
    h = _rms_mod(x_ref[...], g_ref[...], mod[1:2], mod[0:1]).astype(jnp.bfloat16)
    for n0 in range(0, D_A_COLS, 256):
        pa_ref[:, n0:n0 + 256] = _dot(h, w_ref[:, n0:n0 + 256], precision=None)
    for n0 in range(0, 3 * D_B, 256):
        qkv_ref[:, n0:n0 + 256] = _dot(h, w_ref[:, D_A_COLS + n0:D_A_COLS + n0 + 256], precision=None)


def _inproj(x2d, mod, mod_base, rows_per_mod, norm_g, w_in_bf16):
    n = x2d.shape[0]
    tm = 512
    mod_idx = lambda i: (mod_base + (i * tm) // rows_per_mod, 0, 0)
    return pl.pallas_call(
        _inproj_kernel,
        grid=(n // tm,),
        in_specs=[pl.BlockSpec((tm, D_MODEL), lambda i: (i, 0)),
                  pl.BlockSpec((1, 6, D_MODEL), mod_idx),
                  pl.BlockSpec((1, D_MODEL), lambda i: (0, 0)),
                  pl.BlockSpec((D_MODEL, D_IN), lambda i: (0, 0))],
        out_specs=[pl.BlockSpec((tm, D_A_COLS), lambda i: (i, 0)),
                   pl.BlockSpec((tm, 3 * D_B), lambda i: (i, 0))],
        out_shape=[jax.ShapeDtypeStruct((n, D_A_COLS), jnp.float32),
                   jax.ShapeDtypeStruct((n, 3 * D_B), jnp.float32)],
        compiler_params=pltpu.CompilerParams(dimension_semantics=("arbitrary",),
                                             vmem_limit_bytes=VMEM_LIMIT),
        name="inproj",
    )(x2d, mod, norm_g.reshape(1, D_MODEL), w_in_bf16)


def _rwkv_masks():
    L = CHUNK
    i = np.arange(L)[:, None]
    j = np.arange(L)[None, :]
    strict, incl, cmat, levels = [], [], [], []
    for rev in (False, True):
        before = (j > i) if rev else (j < i)
        strict.append(np.tile(before, (1, HEAD_GROUP)))
        incl.append(np.tile(before | (i == j), (1, HEAD_GROUP)))
        cmat.append(before | (i == j))
        lv = []
        s = 1
        while s < L:
            same = (i // (2 * s)) == (j // (2 * s))
            late_i, early_j = (i % (2 * s)) >= s, (j % (2 * s)) < s
            m = same & late_i & early_j
            if rev:
                m = m.T
            lv.append(np.tile(m, (1, HEAD_GROUP)))
            s *= 2
        levels.append(np.stack(lv))
    f = lambda a: jnp.asarray(np.stack(a).astype(np.float32))
    eye = jnp.asarray(np.tile(np.eye(L, dtype=np.float32), (1, HEAD_GROUP)))
    r = np.arange(GROUP_LANES)
    bd = jnp.asarray(((r[:, None] // N_A) == (r[None, :] // N_A)).astype(np.float32))
    return f(strict), f(incl), f(cmat), f(levels), eye, bd


def _softplus(z):
    return jnp.maximum(z, 0.0) + jnp.log1p(jnp.exp(-jnp.abs(z)))


def _rwkv_kernel(pa_ref, s0_ref, w0_ref, wup_ref, a0_ref, aup_ref, kk_ref, ka_ref, rk_ref,
                 strict_ref, incl_ref, cmat_ref, lvl_ref, eye_ref, bd_ref,
                 y_ref, z_ref, sfin_ref, state_ref):
    j = pl.program_id(2)

    @pl.when(j == 0)
    def _():
        state_ref[...] = s0_ref[0, 0]

    L = CHUNK
    r = pa_ref[0, :, 0:D_A]
    k = pa_ref[0, :, D_A:2 * D_A]
    v = pa_ref[0, :, 2 * D_A:3 * D_A]
    lo = pa_ref[0, :, 3 * D_A:3 * D_A + LORA_W + LORA_A]

    wz = w0_ref[0] + _dot(jnp.tanh(lo), wup_ref[0])
    lw = -jnp.exp(-_softplus(-wz) - 0.5)
    a = jax.nn.sigmoid(a0_ref[0] + _dot(lo, aup_ref[0]))
    kd = k * (1.0 + (a - 1.0) * ka_ref[...])
    kkf = k * kk_ref[...]

    strict = strict_ref[0]
    incl = incl_ref[0]
    bd = bd_ref[...]
    eye = eye_ref[...]

    cum = _dot(cmat_ref[0], lw)
    tot = jnp.sum(lw, axis=0, keepdims=True)
    e_in = jnp.exp(cum)
    e_ex = jnp.exp(cum - lw)
    e_inv = jnp.exp(-cum)
    e_fin = jnp.exp(tot - cum)
    g_tot = jnp.exp(tot)

    def expand(m):
        return jnp.concatenate([m] * HEAD_GROUP, axis=0) * bd

    for g in range(H_A // HEAD_GROUP):
        sl = slice(g * GROUP_LANES, (g + 1) * GROUP_LANES)
        kkf_g = kkf[:, sl]
        ss = _dot(kkf_g * kkf_g, bd)
        kk = kkf_g * lax.rsqrt(ss + 1e-12)
        bvec = kk * a[:, sl]
        r_g, kd_g, v_g = r[:, sl], kd[:, sl], v[:, sl]
        a_t = -kk * e_ex[:, sl]
        r_t = r_g * e_in[:, sl]
        b_t = bvec * e_inv[:, sl]
        k_t = kd_g * e_inv[:, sl]
        b_f = bvec * e_fin[:, sl]
        k_f = kd_g * e_fin[:, sl]
        s_old = state_ref[:, sl]

        x_ar = jnp.concatenate([a_t, r_t], axis=0)
        p1 = _dot(x_ar, expand(s_old), _NT)
        xb = _dot(x_ar, expand(b_t), _NT)
        xk = _dot(x_ar, expand(k_t), _NT)
        n_ab = xb[:L] * strict
        a_ak = xk[:L] * strict
        a_rb = xb[L:] * incl
        a_rk = xk[L:] * incl

        t = eye + n_ab * lvl_ref[0, 0]
        for lv in range(1, 6):
            tn = _dot(t, expand(n_ab * lvl_ref[0, lv]))
            t = t + _dot(tn, expand(t))

        bd_v = expand(v_g)
        rhs = p1[:L] + _dot(a_ak, bd_v)
        u = _dot(t, expand(rhs))
        y = p1[L:] + _dot(a_rb, expand(u)) + _dot(a_rk, bd_v)
        q = _dot(jnp.concatenate([u, v_g], axis=0), jnp.concatenate([b_f, k_f], axis=0), _TN) * bd
        s_new = s_old * g_tot[:, sl] + (q[0:N_A] + q[N_A:2 * N_A] + q[2 * N_A:3 * N_A] + q[3 * N_A:4 * N_A])
        state_ref[:, sl] = s_new
        sfin_ref[0, 0, :, sl] = s_new
        y_ref[0, 0, :, sl] = y
        z_ref[0, 0, :, sl] = _dot(r_g * kd_g * rk_ref[:, sl], bd) * v_g


def _rwkv_scan(pa, s0, w0, w_up, a0, a_up, k_k, k_a, r_k):
    b, t, _ = pa.shape
    nc = t // CHUNK
    strict, incl, cmat, levels, eye, bd = _rwkv_masks()
    zpad = jnp.zeros((2, LORA_W, D_A), jnp.float32)
    wup = jnp.concatenate([w_up, zpad], axis=1)
    aup = jnp.concatenate([zpad, a_up], axis=1)
    chunk = lambda d, j: j + d * (nc - 1 - 2 * j)
    dsel3 = lambda bi, d, j: (d, 0, 0)
    const2 = lambda bi, d, j: (0, 0)
    return pl.pallas_call(
        _rwkv_kernel,
        grid=(b, 2, nc),
        in_specs=[pl.BlockSpec((1, CHUNK, D_A_COLS), lambda bi, d, j: (bi, chunk(d, j), 0)),
                  pl.BlockSpec((1, 1, N_A, D_A), lambda bi, d, j: (bi, d, 0, 0)),
                  pl.BlockSpec((1, 1, D_A), dsel3),
                  pl.BlockSpec((1, LORA_W + LORA_A, D_A), dsel3),
                  pl.BlockSpec((1, 1, D_A), dsel3),
                  pl.BlockSpec((1, LORA_W + LORA_A, D_A), dsel3),
                  pl.BlockSpec((1, D_A), const2),
                  pl.BlockSpec((1, D_A), const2),
                  pl.BlockSpec((1, D_A), const2),
                  pl.BlockSpec((1, CHUNK, GROUP_LANES), dsel3),
                  pl.BlockSpec((1, CHUNK, GROUP_LANES), dsel3),
                  pl.BlockSpec((1, CHUNK, CHUNK), dsel3),
                  pl.BlockSpec((1, 6, CHUNK, GROUP_LANES), lambda bi, d, j: (d, 0, 0, 0)),
                  pl.BlockSpec((CHUNK, GROUP_LANES), const2),
                  pl.BlockSpec((GROUP_LANES, GROUP_LANES), const2)],
        out_specs=[pl.BlockSpec((1, 1, CHUNK, D_A), lambda bi, d, j: (d, bi, chunk(d, j), 0)),
                   pl.BlockSpec((1, 1, CHUNK, D_A), lambda bi, d, j: (d, bi, chunk(d, j), 0)),
                   pl.BlockSpec((1, 1, N_A, D_A), lambda bi, d, j: (bi, d, 0, 0))],
        out_shape=[jax.ShapeDtypeStruct((2, b, t, D_A), jnp.float32),
                   jax.ShapeDtypeStruct((2, b, t, D_A), jnp.float32),
                   jax.ShapeDtypeStruct((b, 2, N_A, D_A), jnp.float32)],
        scratch_shapes=[pltpu.VMEM((N_A, D_A), jnp.float32)],
        compiler_params=pltpu.CompilerParams(dimension_semantics=("arbitrary", "arbitrary", "arbitrary"),
                                             vmem_limit_bytes=VMEM_LIMIT),
        name="rwkv_scan",
    )(pa, s0, w0.reshape(2, 1, D_A), wup, a0.reshape(2, 1, D_A), aup,
      k_k.reshape(1, D_A), k_a.reshape(1, D_A), r_k.reshape(1, D_A),
      strict, incl, cmat, levels, eye, bd)


def _outproj_kernel(ya_ref, yb_ref, x_ref, mod_ref, g_ref, w_ref, router_ref, x1_ref, h2_ref, logit_ref):
    mod = mod_ref[0]
    acc = _dot(ya_ref[...].astype(jnp.bfloat16), w_ref[0:D_A, :], precision=None)
    acc = acc + _dot(yb_ref[...].astype(jnp.bfloat16), w_ref[D_A:, :], precision=None)
    x1 = x_ref[...] + mod[2:3] * acc
    x1_ref[...] = x1
    h2 = _rms_mod(x1, g_ref[...], mod[4:5], mod[3:4])
    h2_ref[...] = h2
    logit_ref[...] = _dot(h2, router_ref[...])


def _outproj(ya, yb, x2d, mod, mod_base, rows_per_mod, norm2_g, w_out_bf16, router_pad):
    n = x2d.shape[0]
    tm = 512
    mod_idx = lambda i: (mod_base + (i * tm) // rows_per_mod, 0, 0)
    row = lambda i: (i, 0)
    const = lambda i: (0, 0)
    return pl.pallas_call(
        _outproj_kernel,
        grid=(n // tm,),
        in_specs=[pl.BlockSpec((tm, D_A), row), pl.BlockSpec((tm, D_B), row),
                  pl.BlockSpec((tm, D_MODEL), row),
                  pl.BlockSpec((1, 6, D_MODEL), mod_idx),
                  pl.BlockSpec((1, D_MODEL), const),
                  pl.BlockSpec((D_MODEL, D_MODEL), const),
                  pl.BlockSpec((D_MODEL, 128), const)],
        out_specs=[pl.BlockSpec((tm, D_MODEL), row), pl.BlockSpec((tm, D_MODEL), row),
                   pl.BlockSpec((tm, 128), row)],
        out_shape=[jax.ShapeDtypeStruct((n, D_MODEL), jnp.float32),
                   jax.ShapeDtypeStruct((n, D_MODEL), jnp.float32),
                   jax.ShapeDtypeStruct((n, 128), jnp.float32)],
        compiler_params=pltpu.CompilerParams(dimension_semantics=("arbitrary",),
                                             vmem_limit_bytes=VMEM_LIMIT),
        name="outproj",
    )(ya, yb, x2d, mod, norm2_g.reshape(1, D_MODEL), w_out_bf16, router_pad)


MOE_TF = 256


MOE_TM = 512


def _moe_kernel(xe_ref, wg_ref, wu_ref, wd_ref, o_ref, wg_bf, wu_bf, wd_bf):
    f = pl.program_id(1)
    wg_bf[...] = wg_ref[0].astype(jnp.bfloat16)
    wu_bf[...] = wu_ref[0].astype(jnp.bfloat16)
    wd_bf[...] = wd_ref[0].astype(jnp.bfloat16)

    def rows(i, carry):
        sl = pl.ds(pl.multiple_of(i * MOE_TM, MOE_TM), MOE_TM)
        xe = xe_ref[0, sl, :]
        gt = _dot(xe, wg_bf[...], precision=None)
        up = _dot(xe, wu_bf[...], precision=None)
        hid = (gt * jax.nn.sigmoid(gt) * up).astype(jnp.bfloat16)
        part = _dot(hid, wd_bf[...], precision=None)

        @pl.when(f == 0)
        def _():
            o_ref[0, sl, :] = part

        @pl.when(f > 0)
        def _():
            o_ref[0, sl, :] += part

        return carry

    lax.fori_loop(0, xe_ref.shape[1] // MOE_TM, rows, 0)


def _moe_experts(xe, e_gate, e_up, e_down):
    e, rows, _ = xe.shape
    nf = D_EXPERT // MOE_TF
    return pl.pallas_call(
        _moe_kernel,
        grid=(e, nf),
        in_specs=[pl.BlockSpec((1, rows, D_MODEL), lambda ei, f: (ei, 0, 0)),
                  pl.BlockSpec((1, D_MODEL, MOE_TF), lambda ei, f: (ei, 0, f)),
                  pl.BlockSpec((1, D_MODEL, MOE_TF), lambda ei, f: (ei, 0, f)),
                  pl.BlockSpec((1, MOE_TF, D_MODEL), lambda ei, f: (ei, f, 0))],
        out_specs=pl.BlockSpec((1, rows, D_MODEL), lambda ei, f: (ei, 0, 0)),
        out_shape=jax.ShapeDtypeStruct((e, rows, D_MODEL), jnp.float32),
        scratch_shapes=[pltpu.VMEM((D_MODEL, MOE_TF), jnp.bfloat16),
                        pltpu.VMEM((D_MODEL, MOE_TF), jnp.bfloat16),
                        pltpu.VMEM((MOE_TF, D_MODEL), jnp.bfloat16)],
        compiler_params=pltpu.CompilerParams(dimension_semantics=("arbitrary", "arbitrary"),
                                             vmem_limit_bytes=VMEM_LIMIT),
        name="moe_experts",
    )(xe, e_gate, e_up, e_down)


def _final_kernel(x1_ref, moe_ref, mod_ref, g_ref, o_ref):
    x2 = x1_ref[...] + mod_ref[0][5:6] * moe_ref[...]
    o_ref[...] = x2 * lax.rsqrt(jnp.mean(x2 * x2, axis=-1, keepdims=True) + RMS_EPS) * g_ref[...]


def _final(x1, moe, mod, mod_base, rows_per_mod, final_g):
    n = x1.shape[0]
    tm = 512
    mod_idx = lambda i: (mod_base + (i * tm) // rows_per_mod, 0, 0)
    row = lambda i: (i, 0)
    return pl.pallas_call(
        _final_kernel,
        grid=(n // tm,),
        in_specs=[pl.BlockSpec((tm, D_MODEL), row), pl.BlockSpec((tm, D_MODEL), row),
                  pl.BlockSpec((1, 6, D_MODEL), mod_idx),
                  pl.BlockSpec((1, D_MODEL), lambda i: (0, 0))],
        out_specs=pl.BlockSpec((tm, D_MODEL), row),
        out_shape=jax.ShapeDtypeStruct((n, D_MODEL), jnp.float32),
        compiler_params=pltpu.CompilerParams(dimension_semantics=("arbitrary",),
                                             vmem_limit_bytes=VMEM_LIMIT),
        name="final_norm",
    )(x1, moe, mod, final_g.reshape(1, D_MODEL))


def _centred_shift(p, mu_prev, mu_next):
    zero = jnp.zeros_like(p[:, :1])
    p_prev = jnp.concatenate([zero, p[:, :-1]], axis=1)
    p_next = jnp.concatenate([p[:, 1:], zero], axis=1)
    return p + mu_prev * (p_prev - p) + mu_next * (p_next - p)


def _context_attention(q, k, v):
    b, t, _, _ = q.shape
    s = jnp.einsum('bqhn,bkhn->bhqk', q, k).astype(jnp.float32) * ATTN_SCALE
    p = jax.nn.softmax(s, axis=-1).astype(v.dtype)
    return jnp.einsum('bhqk,bkhn->bqhn', p, v).reshape(b, t, D_B)


def _neighbourhood_attention(q, k, v, ctx_k, ctx_v, rpb):
    b, t, h, n = q.shape
    rows = t // GRID_W
    wr = min(WIN_R, rows)
    r_ids = jnp.arange(rows)
    row_start = jnp.clip(r_ids - wr // 2, 0, rows - wr)
    key_rows = row_start[:, None] + jnp.arange(wr)[None, :]
    kg = k.reshape(b, rows, GRID_W, h, n)[:, key_rows].reshape(b, rows, wr * GRID_W, h, n)
    vg = v.reshape(b, rows, GRID_W, h, n)[:, key_rows].reshape(b, rows, wr * GRID_W, h, n)
    qr = q.reshape(b, rows, GRID_W, h, n)
    s_loc = jnp.einsum('brqhn,brkhn->bhrqk', qr, kg).astype(jnp.float32) * ATTN_SCALE
    c_ids = jnp.arange(GRID_W)
    col_start = jnp.clip(c_ids - WIN_C // 2, 0, GRID_W - WIN_C)
    key_col = jnp.arange(wr * GRID_W) % GRID_W
    mask = (key_col[None, :] >= col_start[:, None]) & (key_col[None, :] < col_start[:, None] + WIN_C)
    dr_idx = key_rows - r_ids[:, None] + WIN_R - 1
    dc_idx = jnp.clip(c_ids[None, :] - c_ids[:, None] + WIN_C - 1, 0, 2 * WIN_C - 2)
    bias = rpb[:, dr_idx[:, None, :, None], dc_idx[None, :, None, :]]
    bias = bias.reshape(h, rows, GRID_W, wr * GRID_W).astype(jnp.float32)
    s_loc = jnp.where(mask, s_loc + bias[None], NEG_INF)
    s_ctx = jnp.einsum('brqhn,blhn->bhrql', qr, ctx_k).astype(jnp.float32) * ATTN_SCALE
    p = jax.nn.softmax(jnp.concatenate([s_loc, s_ctx], axis=-1), axis=-1).astype(v.dtype)
    n_loc = wr * GRID_W
    o = (jnp.einsum('bhrqk,brkhn->brqhn', p[..., :n_loc], vg)
         + jnp.einsum('bhrql,blhn->brqhn', p[..., n_loc:], ctx_v))
    return o.reshape(b, t, D_B)


def _rwkv_post(y, z, pa, v, g_up, gn_g, gn_b):
    b, t, _ = v.shape
    yf = (y[0] + y[1]).reshape(b, t, H_A, N_A)
    mu = jnp.mean(yf, axis=-1, keepdims=True)
    var = jnp.mean(jnp.square(yf - mu), axis=-1, keepdims=True)
    yn = ((yf - mu) * lax.rsqrt(var + GN_EPS)).reshape(b, t, D_A)
    g = jnp.dot(jax.nn.sigmoid(pa[..., D_A_COLS - LORA_G:]), g_up, precision=_HI)
    return (yn * gn_g + gn_b + z[0] + z[1]) * g


def _route(logits, b, t):
    cap = EC_FACTOR * t // N_EXPERTS
    aff = jax.nn.softmax(logits.reshape(b, t, N_EXPERTS), axis=-1)
    gate, idx = lax.top_k(jnp.swapaxes(aff, 1, 2), cap)
    return gate, idx


def kernel(x_prompt, x_sample, cache_na_k, cache_na_v, state_rwkv, c, c_ctx, final_norm_g, norm1_g, norm2_g,
           w_mod, b_mod, w_in, mu_prev, mu_next, w0, w_up, a0, a_up, g_up, k_k, k_a, r_k, gn_g, gn_b, rpb,
           w_out, router, e_gate, e_up, e_down):
    bp, tp, _ = x_prompt.shape
    bs, ts, _ = x_sample.shape
    l = 0
    cond8 = jnp.concatenate([c_ctx[None, :], c, jnp.zeros((8 - 1 - bs, D_MODEL), jnp.float32)], axis=0)
    mod = _adaln(cond8, w_mod[l], b_mod[l]).reshape(8, 6, D_MODEL)
    w_in_bf = w_in[l].astype(jnp.bfloat16)
    w_out_bf = w_out[l].astype(jnp.bfloat16)
    router_pad = jnp.pad(router[l], ((0, 0), (0, 128 - N_EXPERTS)))

    groups = []
    for name, x, mod_base, s0 in (
            ("prompt", x_prompt, 0, jnp.zeros((bp, 2, N_A, D_A), jnp.float32)),
            ("sample", x_sample, 1,
             jnp.transpose(state_rwkv[:, l], (0, 1, 3, 2, 4)).reshape(bs, 2, N_A, D_A))):
        b, t, _ = x.shape
        rows_per_mod = b * t if name == "prompt" else t
        x2d = x.reshape(b * t, D_MODEL)
        pa, qkv = _inproj(x2d, mod, mod_base, rows_per_mod, norm1_g[l], w_in_bf)
        pa = _centred_shift(pa.reshape(b, t, D_A_COLS), mu_prev[l], mu_next[l])
        qkv = qkv.reshape(b, t, 3, H_B, N_B)
        q, k, v = qkv[:, :, 0], qkv[:, :, 1], qkv[:, :, 2]
        y, z, s_fin = _rwkv_scan(pa, s0, w0[l], w_up[l], a0[l], a_up[l], k_k[l], k_a[l], r_k[l].reshape(D_A))
        ya = _rwkv_post(y, z, pa, pa[..., 2 * D_A:3 * D_A], g_up[l], gn_g[l], gn_b[l])
        if name == "prompt":
            yb = _context_attention(q, k, v)
        else:
            yb = _neighbourhood_attention(q, k, v, cache_na_k[:, l], cache_na_v[:, l], rpb[l])
        x1, h2, logits = _outproj(ya.reshape(b * t, D_A), yb.reshape(b * t, D_B), x2d, mod, mod_base,
                                  rows_per_mod, norm2_g[l], w_out_bf, router_pad)
        gate, idx = _route(logits[:, :N_EXPERTS], b, t)
        groups.append(dict(b=b, t=t, x1=x1, h2=h2, gate=gate, idx=idx, k=k, v=v, s_fin=s_fin,
                           mod_base=mod_base, rows_per_mod=rows_per_mod))

    xes = []
    for gr in groups:
        b, t = gr["b"], gr["t"]
        h3 = gr["h2"].astype(jnp.bfloat16).reshape(b, t, D_MODEL)
        xe = h3[jnp.arange(b)[:, None, None], gr["idx"]]
        xes.append(jnp.swapaxes(xe, 0, 1).reshape(N_EXPERTS, -1, D_MODEL))
    xe_all = jnp.concatenate(xes, axis=1)
    ye_all = _moe_experts(xe_all, e_gate[l], e_up[l], e_down[l])

    outs = []
    off = 0
    for gr in groups:
        b, t = gr["b"], gr["t"]
        cap = gr["idx"].shape[-1]
        ye = ye_all[:, off:off + b * cap].reshape(N_EXPERTS, b, cap, D_MODEL)
        off += b * cap
        ye = jnp.swapaxes(ye, 0, 1) * gr["gate"][..., None]
        moe = jnp.zeros((b, t, D_MODEL), jnp.float32).at[jnp.arange(b)[:, None, None], gr["idx"]].add(ye)
        y = _final(gr["x1"], moe.reshape(b * t, D_MODEL), mod, gr["mod_base"], gr["rows_per_mod"], final_norm_g)
        outs.append(y.reshape(b, t, D_MODEL))

    gp = groups[0]
    new_s = jnp.transpose(gp["s_fin"].reshape(bp, 2, N_A, H_A, N_A), (0, 1, 3, 2, 4))
    return (outs[0], outs[1], gp["k"][:, None], gp["v"][:, None], new_s[:, None])
```

```python
import numpy as np
import jax
import jax.numpy as jnp
from jax import lax
from jax.experimental import pallas as pl
from jax.experimental.pallas import tpu as pltpu

D_MODEL = 1024
GRID_W = 64
H_A = 8
N_A = 64
D_A = H_A * N_A
H_B = 8
N_B = 64
D_B = H_B * N_B
LORA_W = 64
LORA_A = 64
LORA_G = 128
D_A_COLS = 3 * D_A + LORA_W + LORA_A + LORA_G
D_IN = D_A_COLS + 3 * D_B
WIN_R = 8
WIN_C = 16
N_EXPERTS = 16
EC_FACTOR = 2
D_EXPERT = 2816
RMS_EPS = 1e-6
GN_EPS = 64e-5
ATTN_SCALE = N_B ** -0.5
NEG_INF = -1e30

CHUNK = 64
HEAD_GROUP = 4
GROUP_LANES = HEAD_GROUP * N_A
ROW_TILE = 512
VMEM_LIMIT = 56 * 1024 * 1024

_NN = (((1,), (0,)), ((), ()))
_NT = (((1,), (1,)), ((), ()))
_TN = (((0,), (0,)), ((), ()))
_BF = jnp.bfloat16


def _dot(a, b, dims=_NN):
    return lax.dot_general(a, b, dims, preferred_element_type=jnp.float32)


def _pieces(a, n):
    out = []
    for i in range(n):
        p = a.astype(_BF)
        out.append(p)
        if i + 1 < n:
            a = a - p.astype(jnp.float32)
    return out


def _mmp(ap, bp, dims=_NN):
    n = max(len(ap), len(bp))
    acc = None
    for i, x in enumerate(ap):
        for j, y in enumerate(bp):
            if i + j < n:
                d = _dot(x, y, dims)
                acc = d if acc is None else acc + d
    return acc


def _mod_kernel(c_ref, w_ref, b_ref, o_ref):
    c = c_ref[...]
    s = c * jax.nn.sigmoid(c)
    o_ref[...] = _dot(s.astype(_BF), w_ref[...].astype(_BF)) + b_ref[...]


def _adaln(cond8, w_mod, b_mod):
    n = w_mod.shape[1]
    tn = 1024
    return pl.pallas_call(
        _mod_kernel,
        grid=(n // tn,),
        in_specs=[pl.BlockSpec((8, D_MODEL), lambda j: (0, 0)),
                  pl.BlockSpec((D_MODEL, tn), lambda j: (0, j)),
                  pl.BlockSpec((1, tn), lambda j: (0, j))],
        out_specs=pl.BlockSpec((8, tn), lambda j: (0, j)),
        out_shape=jax.ShapeDtypeStruct((8, n), jnp.float32),
        compiler_params=pltpu.CompilerParams(dimension_semantics=("arbitrary",),
                                             vmem_limit_bytes=VMEM_LIMIT),
        name="adaln_mod",
    )(cond8, w_mod, b_mod.reshape(1, n))


def _rms_mod(x, g, scale, shift):
    y = x * lax.rsqrt(jnp.mean(x * x, axis=-1, keepdims=True) + RMS_EPS)
    return (y * g) * (1.0 + scale) + shift


def _inproj_kernel(x_ref, mod_ref, g_ref, w_ref, pa_ref, qkv_ref):
    mod = mod_ref[0]
    h = _rms_mod(x_ref[...], g_ref[...], mod[1:2], mod[0:1]).astype(_BF)
    for n0 in range(0, D_A_COLS, 256):
        pa_ref[:, n0:n0 + 256] = _dot(h, w_ref[:, n0:n0 + 256])
    for n0 in range(0, 3 * D_B, 256):
        qkv_ref[:, n0:n0 + 256] = _dot(h, w_ref[:, D_A_COLS + n0:D_A_COLS + n0 + 256]).astype(qkv_ref.dtype)


def _inproj(x2d, mod, mod_base, rows_per_mod, norm_g, w_in_bf16, qkv_dtype):
    n = x2d.shape[0]
    tm = ROW_TILE
    mod_idx = lambda i: (mod_base + (i * tm) // rows_per_mod, 0, 0)
    return pl.pallas_call(
        _inproj_kernel,
        grid=(n // tm,),
        in_specs=[pl.BlockSpec((tm, D_MODEL), lambda i: (i, 0)),
                  pl.BlockSpec((1, 6, D_MODEL), mod_idx),
                  pl.BlockSpec((1, D_MODEL), lambda i: (0, 0)),
                  pl.BlockSpec((D_MODEL, D_IN), lambda i: (0, 0))],
        out_specs=[pl.BlockSpec((tm, D_A_COLS), lambda i: (i, 0)),
                   pl.BlockSpec((tm, 3 * D_B), lambda i: (i, 0))],
        out_shape=[jax.ShapeDtypeStruct((n, D_A_COLS), jnp.float32),
                   jax.ShapeDtypeStruct((n, 3 * D_B), qkv_dtype)],
        compiler_params=pltpu.CompilerParams(dimension_semantics=("arbitrary",),
                                             vmem_limit_bytes=VMEM_LIMIT),
        name="inproj",
    )(x2d, mod, norm_g.reshape(1, D_MODEL), w_in_bf16)


RWKV_PIECES = 1
RWKV_SEQS = 2


def _head_block_mask(n):
    r = np.arange(n)
    return (r[:, None] // N_A) == (r[None, :] // N_A)


def _rwkv_masks():
    L = CHUNK
    i = np.arange(L)[:, None]
    j = np.arange(L)[None, :]
    strict, incl, cmat, levels = [], [], [], []
    for rev in (False, True):
        before = (j > i) if rev else (j < i)
        strict.append(np.tile(before, (1, HEAD_GROUP)))
        incl.append(np.tile(before | (i == j), (1, HEAD_GROUP)))
        cmat.append(before | (i == j))
        lv = []
        s = 1
        while s < L:
            same = (i // (2 * s)) == (j // (2 * s))
            late_i, early_j = (i % (2 * s)) >= s, (j % (2 * s)) < s
            m = same & late_i & early_j
            if rev:
                m = m.T
            lv.append(np.tile(m, (1, HEAD_GROUP)))
            s *= 2
        levels.append(np.stack(lv))
    f = lambda a: jnp.asarray(np.stack(a).astype(np.float32))
    eye = jnp.asarray(np.tile(np.eye(L, dtype=np.float32), (1, HEAD_GROUP)))
    bd = jnp.asarray(_head_block_mask(GROUP_LANES).astype(np.float32)).astype(_BF)
    return f(strict), f(incl), f(cmat).astype(_BF), f(levels), eye, bd


def _softplus(z):
    return jnp.maximum(z, 0.0) + jnp.log1p(jnp.exp(-jnp.abs(z)))


def _rwkv_kernel(pa_ref, s0_ref, w0_ref, wup_ref, a0_ref, aup_ref, kk_ref, ka_ref, rk_ref,
                 strict_ref, incl_ref, cmat_ref, lvl_ref, eye_ref, bd_ref,
                 y_ref, z_ref, sfin_ref, state_ref):
    j = pl.program_id(2)

    @pl.when(j == 0)
    def _():
        state_ref[...] = s0_ref[:, 0]

    L = CHUNK
    NP = RWKV_PIECES
    strict = strict_ref[0]
    incl = incl_ref[0]
    bd = bd_ref[...]
    eye = eye_ref[...]

    def expand(m, n=NP):
        return [jnp.concatenate([p] * HEAD_GROUP, axis=0) * bd for p in _pieces(m, n)]

    chains = []
    for bi in range(pa_ref.shape[0]):
        r = pa_ref[bi, :, 0:D_A]
        k = pa_ref[bi, :, D_A:2 * D_A]
        v = pa_ref[bi, :, 2 * D_A:3 * D_A]
        lo = pa_ref[bi, :, 3 * D_A:3 * D_A + LORA_W + LORA_A]
        wz = w0_ref[0] + _mmp(_pieces(jnp.tanh(lo), 2), _pieces(wup_ref[0], 2))
        lw = -jnp.exp(-_softplus(-wz) - 0.5)
        a = jax.nn.sigmoid(a0_ref[0] + _mmp(_pieces(lo, 2), _pieces(aup_ref[0], 2)))
        kd = k * (1.0 + (a - 1.0) * ka_ref[...])
        kkf = k * kk_ref[...]
        cum = _mmp([cmat_ref[0]], _pieces(lw, 3))
        tot = jnp.sum(lw, axis=0, keepdims=True)
        e_in = jnp.exp(cum)
        e_ex = jnp.exp(cum - lw)
        e_inv = jnp.exp(-cum)
        e_fin = jnp.exp(tot - cum)
        g_tot = jnp.exp(tot)
        for g in range(H_A // HEAD_GROUP):
            sl = slice(g * GROUP_LANES, (g + 1) * GROUP_LANES)
            kkf_g = kkf[:, sl]
            ss = _mmp(_pieces(kkf_g * kkf_g, 2), [bd])
            kk = kkf_g * lax.rsqrt(ss + 1e-12)
            bvec = kk * a[:, sl]
            r_g, kd_g, v_g = r[:, sl], kd[:, sl], v[:, sl]
            chains.append(dict(
                bi=bi, sl=sl, v=v_g, g_tot=g_tot[:, sl],
                x_ar=_pieces(jnp.concatenate([-kk * e_ex[:, sl], r_g * e_in[:, sl]], axis=0), NP),
                b_t=bvec * e_inv[:, sl], k_t=kd_g * e_inv[:, sl],
                bk_f=jnp.concatenate([bvec * e_fin[:, sl], kd_g * e_fin[:, sl]], axis=0),
                z=_mmp(_pieces(r_g * kd_g * rk_ref[:, sl], 2), [bd]) * v_g,
                s_old=state_ref[bi, :, sl]))

    for c in chains:
        c["p1"] = _mmp(c["x_ar"], expand(c["s_old"]), _NT)
        xb = _mmp(c["x_ar"], expand(c["b_t"]), _NT)
        xk = _mmp(c["x_ar"], expand(c["k_t"]), _NT)
        c["n_ab"] = xb[:L] * strict
        c["a_ak"] = xk[:L] * strict
        c["a_rb"] = xb[L:] * incl
        c["a_rk"] = xk[L:] * incl
        c["t"] = eye + c["n_ab"] * lvl_ref[0, 0]
    for lv in range(1, 6):
        for c in chains:
            c["tn"] = _mmp(_pieces(c["t"], NP), expand(c["n_ab"] * lvl_ref[0, lv]))
        for c in chains:
            c["t"] = c["t"] + _mmp(_pieces(c["tn"], NP), expand(c["t"]))
    for c in chains:
        c["bd_v"] = expand(c["v"])
        c["rhs"] = c["p1"][:L] + _mmp(_pieces(c["a_ak"], NP), c["bd_v"])
    for c in chains:
        c["u"] = _mmp(_pieces(c["t"], NP), expand(c["rhs"]))
    for c in chains:
        c["y"] = (c["p1"][L:] + _mmp(_pieces(c["a_rb"], NP), expand(c["u"]))
                  + _mmp(_pieces(c["a_rk"], NP), c["bd_v"]))
        q = _mmp(_pieces(jnp.concatenate([c["u"], c["v"]], axis=0), NP), _pieces(c["bk_f"], NP), _TN) * bd
        c["s_new"] = c["s_old"] * c["g_tot"] + (q[0:N_A] + q[N_A:2 * N_A] + q[2 * N_A:3 * N_A]
                                                + q[3 * N_A:4 * N_A])
    for c in chains:
        bi, sl = c["bi"], c["sl"]
        state_ref[bi, :, sl] = c["s_new"]
        sfin_ref[bi, 0, :, sl] = c["s_new"]
        y_ref[0, bi, :, sl] = c["y"]
        z_ref[0, bi, :, sl] = c["z"]


def _rwkv_scan(pa, s0, w0, w_up, a0, a_up, k_k, k_a, r_k):
    b, t, _ = pa.shape
    nc = t // CHUNK
    strict, incl, cmat, levels, eye, bd = _rwkv_masks()
    zpad = jnp.zeros((2, LORA_W, D_A), jnp.float32)
    wup = jnp.concatenate([w_up, zpad], axis=1)
    aup = jnp.concatenate([zpad, a_up], axis=1)
    chunk = lambda d, j: j + d * (nc - 1 - 2 * j)
    dsel3 = lambda bi, d, j: (d, 0, 0)
    const2 = lambda bi, d, j: (0, 0)
    nb = RWKV_SEQS
    return pl.pallas_call(
        _rwkv_kernel,
        grid=(b // nb, 2, nc),
        in_specs=[pl.BlockSpec((nb, CHUNK, D_A_COLS), lambda bi, d, j: (bi, chunk(d, j), 0)),
                  pl.BlockSpec((nb, 1, N_A, D_A), lambda bi, d, j: (bi, d, 0, 0)),
                  pl.BlockSpec((1, 1, D_A), dsel3),
                  pl.BlockSpec((1, LORA_W + LORA_A, D_A), dsel3),
                  pl.BlockSpec((1, 1, D_A), dsel3),
                  pl.BlockSpec((1, LORA_W + LORA_A, D_A), dsel3),
                  pl.BlockSpec((1, D_A), const2),
                  pl.BlockSpec((1, D_A), const2),
                  pl.BlockSpec((1, D_A), const2),
                  pl.BlockSpec((1, CHUNK, GROUP_LANES), dsel3),
                  pl.BlockSpec((1, CHUNK, GROUP_LANES), dsel3),
                  pl.BlockSpec((1, CHUNK, CHUNK), dsel3),
                  pl.BlockSpec((1, 6, CHUNK, GROUP_LANES), lambda bi, d, j: (d, 0, 0, 0)),
                  pl.BlockSpec((CHUNK, GROUP_LANES), const2),
                  pl.BlockSpec((GROUP_LANES, GROUP_LANES), const2)],
        out_specs=[pl.BlockSpec((1, nb, CHUNK, D_A), lambda bi, d, j: (d, bi, chunk(d, j), 0)),
                   pl.BlockSpec((1, nb, CHUNK, D_A), lambda bi, d, j: (d, bi, chunk(d, j), 0)),
                   pl.BlockSpec((nb, 1, N_A, D_A), lambda bi, d, j: (bi, d, 0, 0))],
        out_shape=[jax.ShapeDtypeStruct((2, b, t, D_A), jnp.float32),
                   jax.ShapeDtypeStruct((2, b, t, D_A), jnp.float32),
                   jax.ShapeDtypeStruct((b, 2, N_A, D_A), jnp.float32)],
        scratch_shapes=[pltpu.VMEM((nb, N_A, D_A), jnp.float32)],
        compiler_params=pltpu.CompilerParams(dimension_semantics=("arbitrary", "arbitrary", "arbitrary"),
                                             vmem_limit_bytes=VMEM_LIMIT),
        name="rwkv_scan",
    )(pa, s0, w0.reshape(2, 1, D_A), wup, a0.reshape(2, 1, D_A), aup,
      k_k.reshape(1, D_A), k_a.reshape(1, D_A), r_k.reshape(1, D_A),
      strict, incl, cmat, levels, eye, bd)


def _softmax_pv(scores, values):
    m = scores[0].max(axis=-1, keepdims=True)
    for s in scores[1:]:
        m = jnp.maximum(m, s.max(axis=-1, keepdims=True))
    den = None
    acc = None
    for s, val in zip(scores, values):
        p = jnp.exp(s - m)
        d = p.sum(axis=-1, keepdims=True)
        o = _dot(p.astype(_BF), val)
        den = d if den is None else den + d
        acc = o if acc is None else acc + o
    return acc / den


def _ctx_attn_kernel(qkv_ref, o_ref):
    for h in range(H_B):
        hs = slice(h * N_B, (h + 1) * N_B)
        q = (qkv_ref[0, :, hs] * ATTN_SCALE).astype(_BF)
        k = qkv_ref[0, :, D_B + h * N_B:D_B + (h + 1) * N_B].astype(_BF)
        v = qkv_ref[0, :, 2 * D_B + h * N_B:2 * D_B + (h + 1) * N_B].astype(_BF)
        o_ref[0, :, hs] = _softmax_pv([_dot(q, k, _NT)], [v])


def _context_attention(qkv):
    b, t, _ = qkv.shape
    return pl.pallas_call(
        _ctx_attn_kernel,
        grid=(b,),
        in_specs=[pl.BlockSpec((1, t, 3 * D_B), lambda i: (i, 0, 0))],
        out_specs=pl.BlockSpec((1, t, D_B), lambda i: (i, 0, 0)),
        out_shape=jax.ShapeDtypeStruct((b, t, D_B), jnp.float32),
        compiler_params=pltpu.CompilerParams(dimension_semantics=("arbitrary",),
                                             vmem_limit_bytes=VMEM_LIMIT),
        name="context_attention",
    )(qkv)


NA_BAND = WIN_R * GRID_W
NA_EDGE = WIN_R // 2


def _na_bias(rpb, rows):
    cfg_rows = list(range(NA_EDGE)) + [NA_EDGE] + list(range(rows - NA_EDGE, rows))
    r_ids = np.asarray(cfg_rows)
    row_start = np.clip(r_ids - WIN_R // 2, 0, rows - WIN_R)
    dr = row_start[:, None] + np.arange(WIN_R)[None, :] - r_ids[:, None] + WIN_R - 1
    c_ids = np.arange(GRID_W)
    col_start = np.clip(c_ids - WIN_C // 2, 0, GRID_W - WIN_C)
    valid = (c_ids[None, :] >= col_start[:, None]) & (c_ids[None, :] < col_start[:, None] + WIN_C)
    dc = np.clip(c_ids[None, :] - c_ids[:, None] + WIN_C - 1, 0, 2 * WIN_C - 2)
    bias = rpb[:, dr[:, None, :, None], dc[None, :, None, :]]
    bias = jnp.where(valid[None, None, :, None, :], bias, NEG_INF)
    return jnp.transpose(bias, (1, 0, 2, 3, 4)).reshape(len(cfg_rows), H_B, GRID_W, NA_BAND)


def _na_kernel(q_ref, k_ref, v_ref, ck_ref, cv_ref, bias_ref, o_ref):
    r = pl.program_id(1)
    rows = k_ref.shape[1] // GRID_W
    start = pl.multiple_of(jnp.clip(r - WIN_R // 2, 0, rows - WIN_R) * GRID_W, GRID_W)
    for h in range(H_B):
        hs = slice(h * N_B, (h + 1) * N_B)
        q = q_ref[0, :, hs] * ATTN_SCALE
        k = k_ref[0, pl.ds(start, NA_BAND), hs]
        v = v_ref[0, pl.ds(start, NA_BAND), hs]
        s_loc = _dot(q, k, _NT) + bias_ref[0, h]
        s_ctx = _dot(q, ck_ref[0, :, hs], _NT)
        o_ref[0, :, hs] = _softmax_pv([s_loc, s_ctx], [v, cv_ref[0, :, hs]])


def _neighbourhood_attention(qkv_bf16, ctx_k, ctx_v, rpb):
    b, t, _ = qkv_bf16.shape
    rows = t // GRID_W
    past = ctx_k.shape[1]
    bias = _na_bias(rpb, rows)
    cfg = lambda bi, r: (jnp.minimum(r, NA_EDGE) + jnp.maximum(r - (rows - NA_EDGE - 1), 0), 0, 0, 0)
    return pl.pallas_call(
        _na_kernel,
        grid=(b, rows),
        in_specs=[pl.BlockSpec((1, GRID_W, D_B), lambda bi, r: (bi, r, 0)),
                  pl.BlockSpec((1, t, D_B), lambda bi, r: (bi, 0, 1)),
                  pl.BlockSpec((1, t, D_B), lambda bi, r: (bi, 0, 2)),
                  pl.BlockSpec((1, past, D_B), lambda bi, r: (bi, 0, 0)),
                  pl.BlockSpec((1, past, D_B), lambda bi, r: (bi, 0, 0)),
                  pl.BlockSpec((1, H_B, GRID_W, NA_BAND), cfg)],
        out_specs=pl.BlockSpec((1, GRID_W, D_B), lambda bi, r: (bi, r, 0)),
        out_shape=jax.ShapeDtypeStruct((b, t, D_B), jnp.float32),
        compiler_params=pltpu.CompilerParams(dimension_semantics=("arbitrary", "arbitrary"),
                                             vmem_limit_bytes=VMEM_LIMIT),
        name="neighbourhood_attention",
    )(qkv_bf16, qkv_bf16, qkv_bf16, ctx_k.reshape(b, past, D_B).astype(_BF),
      ctx_v.reshape(b, past, D_B).astype(_BF), bias)


def _outproj_kernel(y_ref, z_ref, glo_ref, yb_ref, x_ref, mod_ref, g2_ref, w_ref, router_ref,
                    gup_ref, gng_ref, gnb_ref, bd_ref, x1_ref, h2_ref, logit_ref):
    mod = mod_ref[0]
    bd = bd_ref[...]
    yf = y_ref[0] + y_ref[1]
    mu = _mmp(_pieces(yf, 3), [bd]) * (1.0 / N_A)
    dev = yf - mu
    var = _mmp(_pieces(dev * dev, 3), [bd]) * (1.0 / N_A)
    yn = dev * lax.rsqrt(var + GN_EPS)
    gate = _mmp(_pieces(jax.nn.sigmoid(glo_ref[...]), 2), _pieces(gup_ref[...], 2))
    ya = (yn * gng_ref[...] + gnb_ref[...] + z_ref[0] + z_ref[1]) * gate
    acc = _dot(ya.astype(_BF), w_ref[0:D_A, :]) + _dot(yb_ref[...].astype(_BF), w_ref[D_A:, :])
    x1 = x_ref[...] + mod[2:3] * acc
    x1_ref[...] = x1
    h2 = _rms_mod(x1, g2_ref[...], mod[4:5], mod[3:4])
    h2_ref[...] = h2.astype(_BF)
    logit_ref[...] = _mmp(_pieces(h2, 2), _pieces(router_ref[...], 2))


def _outproj(y, z, pa2d, yb, x2d, mod, mod_base, rows_per_mod, norm2_g, w_out_bf16, router_pad,
             g_up, gn_g, gn_b):
    n = x2d.shape[0]
    tm = ROW_TILE
    mod_idx = lambda i: (mod_base + (i * tm) // rows_per_mod, 0, 0)
    row = lambda i: (i, 0)
    row3 = lambda i: (0, i, 0)
    const = lambda i: (0, 0)
    bd = jnp.asarray(_head_block_mask(D_A).astype(np.float32)).astype(_BF)
    return pl.pallas_call(
        _outproj_kernel,
        grid=(n // tm,),
        in_specs=[pl.BlockSpec((2, tm, D_A), row3), pl.BlockSpec((2, tm, D_A), row3),
                  pl.BlockSpec((tm, LORA_G), lambda i: (i, (D_A_COLS - LORA_G) // LORA_G)),
                  pl.BlockSpec((tm, D_B), row),
                  pl.BlockSpec((tm, D_MODEL), row),
                  pl.BlockSpec((1, 6, D_MODEL), mod_idx),
                  pl.BlockSpec((1, D_MODEL), const),
                  pl.BlockSpec((D_MODEL, D_MODEL), const),
                  pl.BlockSpec((D_MODEL, 128), const),
                  pl.BlockSpec((LORA_G, D_A), const),
                  pl.BlockSpec((1, D_A), const),
                  pl.BlockSpec((1, D_A), const),
                  pl.BlockSpec((D_A, D_A), const)],
        out_specs=[pl.BlockSpec((tm, D_MODEL), row), pl.BlockSpec((tm, D_MODEL), row),
                   pl.BlockSpec((tm, 128), row)],
        out_shape=[jax.ShapeDtypeStruct((n, D_MODEL), jnp.float32),
                   jax.ShapeDtypeStruct((n, D_MODEL), _BF),
                   jax.ShapeDtypeStruct((n, 128), jnp.float32)],
        compiler_params=pltpu.CompilerParams(dimension_semantics=("arbitrary",),
                                             vmem_limit_bytes=VMEM_LIMIT),
        name="outproj",
    )(y, z, pa2d, yb, x2d, mod, norm2_g.reshape(1, D_MODEL), w_out_bf16, router_pad,
      g_up, gn_g.reshape(1, D_A), gn_b.reshape(1, D_A), bd)


MOE_TF = 256
MOE_TM = 512


def _moe_kernel(xe_ref, wg_ref, wu_ref, wd_ref, o_ref, wg_bf, wu_bf, wd_bf):
    f = pl.program_id(1)
    wg_bf[...] = wg_ref[0].astype(_BF)
    wu_bf[...] = wu_ref[0].astype(_BF)
    wd_bf[...] = wd_ref[0].astype(_BF)

    def rows(i, carry):
        sl = pl.ds(pl.multiple_of(i * MOE_TM, MOE_TM), MOE_TM)
        xe = xe_ref[0, sl, :]
        gt = _dot(xe, wg_bf[...])
        up = _dot(xe, wu_bf[...])
        hid = (gt * jax.nn.sigmoid(gt) * up).astype(_BF)
        part = _dot(hid, wd_bf[...])

        @pl.when(f == 0)
        def _():
            o_ref[0, sl, :] = part

        @pl.when(f > 0)
        def _():
            o_ref[0, sl, :] += part

        return carry

    lax.fori_loop(0, xe_ref.shape[1] // MOE_TM, rows, 0)


def _moe_experts(xe, e_gate, e_up, e_down):
    e, rows, _ = xe.shape
    nf = D_EXPERT // MOE_TF
    return pl.pallas_call(
        _moe_kernel,
        grid=(e, nf),
        in_specs=[pl.BlockSpec((1, rows, D_MODEL), lambda ei, f: (ei, 0, 0)),
                  pl.BlockSpec((1, D_MODEL, MOE_TF), lambda ei, f: (ei, 0, f)),
                  pl.BlockSpec((1, D_MODEL, MOE_TF), lambda ei, f: (ei, 0, f)),
                  pl.BlockSpec((1, MOE_TF, D_MODEL), lambda ei, f: (ei, f, 0))],
        out_specs=pl.BlockSpec((1, rows, D_MODEL), lambda ei, f: (ei, 0, 0)),
        out_shape=jax.ShapeDtypeStruct((e, rows, D_MODEL), jnp.float32),
        scratch_shapes=[pltpu.VMEM((D_MODEL, MOE_TF), _BF),
                        pltpu.VMEM((D_MODEL, MOE_TF), _BF),
                        pltpu.VMEM((MOE_TF, D_MODEL), _BF)],
        compiler_params=pltpu.CompilerParams(dimension_semantics=("arbitrary", "arbitrary"),
                                             vmem_limit_bytes=VMEM_LIMIT),
        name="moe_experts",
    )(xe, e_gate, e_up, e_down)


def _final_kernel(x1_ref, moe_ref, mod_ref, g_ref, o_ref):
    x2 = x1_ref[...] + mod_ref[0][5:6] * moe_ref[...]
    o_ref[...] = x2 * lax.rsqrt(jnp.mean(x2 * x2, axis=-1, keepdims=True) + RMS_EPS) * g_ref[...]


def _final(x1, moe, mod, mod_base, rows_per_mod, final_g):
    n = x1.shape[0]
    tm = ROW_TILE
    mod_idx = lambda i: (mod_base + (i * tm) // rows_per_mod, 0, 0)
    row = lambda i: (i, 0)
    return pl.pallas_call(
        _final_kernel,
        grid=(n // tm,),
        in_specs=[pl.BlockSpec((tm, D_MODEL), row), pl.BlockSpec((tm, D_MODEL), row),
                  pl.BlockSpec((1, 6, D_MODEL), mod_idx),
                  pl.BlockSpec((1, D_MODEL), lambda i: (0, 0))],
        out_specs=pl.BlockSpec((tm, D_MODEL), row),
        out_shape=jax.ShapeDtypeStruct((n, D_MODEL), jnp.float32),
        compiler_params=pltpu.CompilerParams(dimension_semantics=("arbitrary",),
                                             vmem_limit_bytes=VMEM_LIMIT),
        name="final_norm",
    )(x1, moe, mod, final_g.reshape(1, D_MODEL))


def _centred_shift(p, mu_prev, mu_next):
    zero = jnp.zeros_like(p[:, :1])
    p_prev = jnp.concatenate([zero, p[:, :-1]], axis=1)
    p_next = jnp.concatenate([p[:, 1:], zero], axis=1)
    return p + mu_prev * (p_prev - p) + mu_next * (p_next - p)


def _route(logits, b, t):
    cap = EC_FACTOR * t // N_EXPERTS
    aff = jax.nn.softmax(logits.reshape(b, t, N_EXPERTS), axis=-1)
    gate, idx = lax.top_k(jnp.swapaxes(aff, 1, 2), cap)
    return gate, idx


def kernel(x_prompt, x_sample, cache_na_k, cache_na_v, state_rwkv, c, c_ctx, final_norm_g, norm1_g, norm2_g,
           w_mod, b_mod, w_in, mu_prev, mu_next, w0, w_up, a0, a_up, g_up, k_k, k_a, r_k, gn_g, gn_b, rpb,
           w_out, router, e_gate, e_up, e_down):
    bp, tp, _ = x_prompt.shape
    bs, ts, _ = x_sample.shape
    l = 0
    cond8 = jnp.concatenate([c_ctx[None, :], c, jnp.zeros((8 - 1 - bs, D_MODEL), jnp.float32)], axis=0)
    mod = _adaln(cond8, w_mod[l], b_mod[l]).reshape(8, 6, D_MODEL)
    w_in_bf = w_in[l].astype(_BF)
    w_out_bf = w_out[l].astype(_BF)
    router_pad = jnp.pad(router[l], ((0, 0), (0, 128 - N_EXPERTS)))

    groups = []
    for name, x, mod_base, s0 in (
            ("prompt", x_prompt, 0, jnp.zeros((bp, 2, N_A, D_A), jnp.float32)),
            ("sample", x_sample, 1,
             jnp.transpose(state_rwkv[:, l], (0, 1, 3, 2, 4)).reshape(bs, 2, N_A, D_A))):
        b, t, _ = x.shape
        is_prompt = name == "prompt"
        rows_per_mod = b * t if is_prompt else t
        x2d = x.reshape(b * t, D_MODEL)
        pa, qkv = _inproj(x2d, mod, mod_base, rows_per_mod, norm1_g[l], w_in_bf,
                          jnp.float32 if is_prompt else _BF)
        pa = _centred_shift(pa.reshape(b, t, D_A_COLS), mu_prev[l], mu_next[l])
        qkv = qkv.reshape(b, t, 3 * D_B)
        y, z, s_fin = _rwkv_scan(pa, s0, w0[l], w_up[l], a0[l], a_up[l], k_k[l], k_a[l], r_k[l].reshape(D_A))
        if is_prompt:
            yb = _context_attention(qkv)
        else:
            yb = _neighbourhood_attention(qkv, cache_na_k[:, l], cache_na_v[:, l], rpb[l])
        x1, h2, logits = _outproj(y.reshape(2, b * t, D_A), z.reshape(2, b * t, D_A),
                                  pa.reshape(b * t, D_A_COLS), yb.reshape(b * t, D_B), x2d, mod, mod_base,
                                  rows_per_mod, norm2_g[l], w_out_bf, router_pad, g_up[l], gn_g[l], gn_b[l])
        gate, idx = _route(logits[:, :N_EXPERTS], b, t)
        groups.append(dict(b=b, t=t, x1=x1, h2=h2, gate=gate, idx=idx, qkv=qkv, s_fin=s_fin,
                           mod_base=mod_base, rows_per_mod=rows_per_mod))

    xes = []
    for gr in groups:
        b, t = gr["b"], gr["t"]
        h3 = gr["h2"].reshape(b, t, D_MODEL)
        xe = h3[jnp.arange(b)[:, None, None], gr["idx"]]
        xes.append(jnp.swapaxes(xe, 0, 1).reshape(N_EXPERTS, -1, D_MODEL))
    xe_all = jnp.concatenate(xes, axis=1)
    ye_all = _moe_experts(xe_all, e_gate[l], e_up[l], e_down[l])

    outs = []
    off = 0
    for gr in groups:
        b, t = gr["b"], gr["t"]
        cap = gr["idx"].shape[-1]
        ye = ye_all[:, off:off + b * cap].reshape(N_EXPERTS, b, cap, D_MODEL)
        off += b * cap
        ye = jnp.swapaxes(ye, 0, 1) * gr["gate"][..., None]
        moe = jnp.zeros((b, t, D_MODEL), jnp.float32).at[jnp.arange(b)[:, None, None], gr["idx"]].add(ye)
        y = _final(gr["x1"], moe.reshape(b * t, D_MODEL), mod, gr["mod_base"], gr["rows_per_mod"], final_norm_g)
        outs.append(y.reshape(b, t, D_MODEL))

    gp = groups[0]
    new_k = gp["qkv"][:, :, D_B:2 * D_B].reshape(bp, 1, tp, H_B, N_B)
    new_v = gp["qkv"][:, :, 2 * D_B:].reshape(bp, 1, tp, H_B, N_B)
    new_s = jnp.transpose(gp["s_fin"].reshape(bp, 2, N_A, H_A, N_A), (0, 1, 3, 2, 4))
    return (outs[0], outs[1], new_k, new_v, new_s[:, None])
```

```python
import numpy as np
import jax
import jax.numpy as jnp
from jax import lax
from jax.experimental import pallas as pl
from jax.experimental.pallas import tpu as pltpu

D_MODEL = 1024
GRID_W = 64
H_A = 8
N_A = 64
D_A = H_A * N_A
H_B = 8
N_B = 64
D_B = H_B * N_B
LORA_W = 64
LORA_A = 64
LORA_G = 128
D_A_COLS = 3 * D_A + LORA_W + LORA_A + LORA_G
D_IN = D_A_COLS + 3 * D_B
WIN_R = 8
WIN_C = 16
N_EXPERTS = 16
EC_FACTOR = 2
D_EXPERT = 2816
RMS_EPS = 1e-6
GN_EPS = 64e-5
ATTN_SCALE = N_B ** -0.5
NEG_INF = -1e30

CHUNK = 64
HEAD_GROUP = 4
GROUP_LANES = HEAD_GROUP * N_A
ROW_TILE = 512
VMEM_LIMIT = 56 * 1024 * 1024

_NN = (((1,), (0,)), ((), ()))
_NT = (((1,), (1,)), ((), ()))
_TN = (((0,), (0,)), ((), ()))
_BF = jnp.bfloat16


def _dot(a, b, dims=_NN):
    return lax.dot_general(a, b, dims, preferred_element_type=jnp.float32)


def _pieces(a, n):
    out = []
    for i in range(n):
        p = a.astype(_BF)
        out.append(p)
        if i + 1 < n:
            a = a - p.astype(jnp.float32)
    return out


def _mmp(ap, bp, dims=_NN):
    n = max(len(ap), len(bp))
    acc = None
    for i, x in enumerate(ap):
        for j, y in enumerate(bp):
            if i + j < n:
                d = _dot(x, y, dims)
                acc = d if acc is None else acc + d
    return acc


def _mod_kernel(c_ref, w_ref, b_ref, o_ref):
    c = c_ref[...]
    s = c * jax.nn.sigmoid(c)
    o_ref[...] = _dot(s.astype(_BF), w_ref[...].astype(_BF)) + b_ref[...]


def _adaln(cond8, w_mod, b_mod):
    n = w_mod.shape[1]
    tn = 1024
    return pl.pallas_call(
        _mod_kernel,
        grid=(n // tn,),
        in_specs=[pl.BlockSpec((8, D_MODEL), lambda j: (0, 0)),
                  pl.BlockSpec((D_MODEL, tn), lambda j: (0, j)),
                  pl.BlockSpec((1, tn), lambda j: (0, j))],
        out_specs=pl.BlockSpec((8, tn), lambda j: (0, j)),
        out_shape=jax.ShapeDtypeStruct((8, n), jnp.float32),
        compiler_params=pltpu.CompilerParams(dimension_semantics=("arbitrary",),
                                             vmem_limit_bytes=VMEM_LIMIT),
        name="adaln_mod",
    )(cond8, w_mod, b_mod.reshape(1, n))


def _rms_mod(x, g, scale, shift):
    y = x * lax.rsqrt(jnp.mean(x * x, axis=-1, keepdims=True) + RMS_EPS)
    return (y * g) * (1.0 + scale) + shift


def _inproj_kernel(x_ref, mod_ref, g_ref, w_ref, pa_ref, qkv_ref):
    mod = mod_ref[0]
    h = _rms_mod(x_ref[...], g_ref[...], mod[1:2], mod[0:1]).astype(_BF)
    for n0 in range(0, D_A_COLS, 256):
        pa_ref[:, n0:n0 + 256] = _dot(h, w_ref[:, n0:n0 + 256])
    for n0 in range(0, 3 * D_B, 256):
        qkv_ref[:, n0:n0 + 256] = _dot(h, w_ref[:, D_A_COLS + n0:D_A_COLS + n0 + 256]).astype(qkv_ref.dtype)


def _inproj(x2d, mod, mod_base, rows_per_mod, norm_g, w_in_bf16, qkv_dtype):
    n = x2d.shape[0]
    tm = ROW_TILE
    mod_idx = lambda i: (mod_base + (i * tm) // rows_per_mod, 0, 0)
    return pl.pallas_call(
        _inproj_kernel,
        grid=(n // tm,),
        in_specs=[pl.BlockSpec((tm, D_MODEL), lambda i: (i, 0)),
                  pl.BlockSpec((1, 6, D_MODEL), mod_idx),
                  pl.BlockSpec((1, D_MODEL), lambda i: (0, 0)),
                  pl.BlockSpec((D_MODEL, D_IN), lambda i: (0, 0))],
        out_specs=[pl.BlockSpec((tm, D_A_COLS), lambda i: (i, 0)),
                   pl.BlockSpec((tm, 3 * D_B), lambda i: (i, 0))],
        out_shape=[jax.ShapeDtypeStruct((n, D_A_COLS), jnp.float32),
                   jax.ShapeDtypeStruct((n, 3 * D_B), qkv_dtype)],
        compiler_params=pltpu.CompilerParams(dimension_semantics=("arbitrary",),
                                             vmem_limit_bytes=VMEM_LIMIT),
        name="inproj",
    )(x2d, mod, norm_g.reshape(1, D_MODEL), w_in_bf16)


RWKV_PIECES = 1
RWKV_SEQS = 2


def _head_block_mask(n):
    r = np.arange(n)
    return (r[:, None] // N_A) == (r[None, :] // N_A)


def _rwkv_masks():
    L = CHUNK
    i = np.arange(L)[:, None]
    j = np.arange(L)[None, :]
    strict, incl, cmat, levels = [], [], [], []
    for rev in (False, True):
        before = (j > i) if rev else (j < i)
        strict.append(np.tile(before, (1, HEAD_GROUP)))
        incl.append(np.tile(before | (i == j), (1, HEAD_GROUP)))
        cmat.append(before | (i == j))
        lv = []
        s = 1
        while s < L:
            same = (i // (2 * s)) == (j // (2 * s))
            late_i, early_j = (i % (2 * s)) >= s, (j % (2 * s)) < s
            m = same & late_i & early_j
            if rev:
                m = m.T
            lv.append(np.tile(m, (1, HEAD_GROUP)))
            s *= 2
        levels.append(np.stack(lv))
    f = lambda a: jnp.asarray(np.stack(a).astype(np.float32))
    eye = jnp.asarray(np.tile(np.eye(L, dtype=np.float32), (1, HEAD_GROUP)))
    bd = jnp.asarray(_head_block_mask(GROUP_LANES).astype(np.float32)).astype(_BF)
    return f(strict), f(incl), f(cmat).astype(_BF), f(levels), eye, bd


def _softplus(z):
    return jnp.maximum(z, 0.0) + jnp.log1p(jnp.exp(-jnp.abs(z)))


def _rwkv_kernel(pa_ref, s0_ref, w0_ref, wup_ref, a0_ref, aup_ref, kk_ref, ka_ref, rk_ref,
                 strict_ref, incl_ref, cmat_ref, lvl_ref, eye_ref, bd_ref,
                 y_ref, z_ref, sfin_ref, state_ref):
    j = pl.program_id(2)

    @pl.when(j == 0)
    def _():
        state_ref[...] = s0_ref[:, 0]

    L = CHUNK
    NP = RWKV_PIECES
    strict = strict_ref[0]
    incl = incl_ref[0]
    bd = bd_ref[...]
    eye = eye_ref[...]

    def expand(m, n=NP):
        return [jnp.concatenate([p] * HEAD_GROUP, axis=0) * bd for p in _pieces(m, n)]

    chains = []
    for bi in range(pa_ref.shape[0]):
        r = pa_ref[bi, :, 0:D_A]
        k = pa_ref[bi, :, D_A:2 * D_A]
        v = pa_ref[bi, :, 2 * D_A:3 * D_A]
        lo = pa_ref[bi, :, 3 * D_A:3 * D_A + LORA_W + LORA_A]
        wz = w0_ref[0] + _mmp(_pieces(jnp.tanh(lo), 2), _pieces(wup_ref[0], 2))
        lw = -jnp.exp(-_softplus(-wz) - 0.5)
        a = jax.nn.sigmoid(a0_ref[0] + _mmp(_pieces(lo, 2), _pieces(aup_ref[0], 2)))
        kd = k * (1.0 + (a - 1.0) * ka_ref[...])
        kkf = k * kk_ref[...]
        cum = _mmp([cmat_ref[0]], _pieces(lw, 3))
        tot = jnp.sum(lw, axis=0, keepdims=True)
        e_in = jnp.exp(cum)
        e_ex = jnp.exp(cum - lw)
        e_inv = jnp.exp(-cum)
        e_fin = jnp.exp(tot - cum)
        g_tot = jnp.exp(tot)
        for g in range(H_A // HEAD_GROUP):
            sl = slice(g * GROUP_LANES, (g + 1) * GROUP_LANES)
            kkf_g = kkf[:, sl]
            ss = _mmp(_pieces(kkf_g * kkf_g, 2), [bd])
            kk = kkf_g * lax.rsqrt(ss + 1e-12)
            bvec = kk * a[:, sl]
            r_g, kd_g, v_g = r[:, sl], kd[:, sl], v[:, sl]
            chains.append(dict(
                bi=bi, sl=sl, v=v_g, g_tot=g_tot[:, sl],
                x_ar=_pieces(jnp.concatenate([-kk * e_ex[:, sl], r_g * e_in[:, sl]], axis=0), NP),
                b_t=bvec * e_inv[:, sl], k_t=kd_g * e_inv[:, sl],
                bk_f=jnp.concatenate([bvec * e_fin[:, sl], kd_g * e_fin[:, sl]], axis=0),
                z=_mmp(_pieces(r_g * kd_g * rk_ref[:, sl], 2), [bd]) * v_g,
                s_old=state_ref[bi, :, sl]))

    for c in chains:
        c["p1"] = _mmp(c["x_ar"], expand(c["s_old"]), _NT)
        xb = _mmp(c["x_ar"], expand(c["b_t"]), _NT)
        xk = _mmp(c["x_ar"], expand(c["k_t"]), _NT)
        c["n_ab"] = xb[:L] * strict
        c["a_ak"] = xk[:L] * strict
        c["a_rb"] = xb[L:] * incl
        c["a_rk"] = xk[L:] * incl
        c["t"] = eye + c["n_ab"] * lvl_ref[0, 0]
    for lv in range(1, 6):
        for c in chains:
            c["tn"] = _mmp(_pieces(c["t"], NP), expand(c["n_ab"] * lvl_ref[0, lv]))
        for c in chains:
            c["t"] = c["t"] + _mmp(_pieces(c["tn"], NP), expand(c["t"]))
    for c in chains:
        c["bd_v"] = expand(c["v"])
        c["rhs"] = c["p1"][:L] + _mmp(_pieces(c["a_ak"], NP), c["bd_v"])
    for c in chains:
        c["u"] = _mmp(_pieces(c["t"], NP), expand(c["rhs"]))
    for c in chains:
        c["y"] = (c["p1"][L:] + _mmp(_pieces(c["a_rb"], NP), expand(c["u"]))
                  + _mmp(_pieces(c["a_rk"], NP), c["bd_v"]))
        q = _mmp(_pieces(jnp.concatenate([c["u"], c["v"]], axis=0), NP), _pieces(c["bk_f"], NP), _TN) * bd
        c["s_new"] = c["s_old"] * c["g_tot"] + (q[0:N_A] + q[N_A:2 * N_A] + q[2 * N_A:3 * N_A]
                                                + q[3 * N_A:4 * N_A])
    for c in chains:
        bi, sl = c["bi"], c["sl"]
        state_ref[bi, :, sl] = c["s_new"]
        sfin_ref[bi, 0, :, sl] = c["s_new"]
        y_ref[0, bi, :, sl] = c["y"]
        z_ref[0, bi, :, sl] = c["z"]


def _rwkv_scan(pa, s0, w0, w_up, a0, a_up, k_k, k_a, r_k):
    b, t, _ = pa.shape
    nc = t // CHUNK
    strict, incl, cmat, levels, eye, bd = _rwkv_masks()
    zpad = jnp.zeros((2, LORA_W, D_A), jnp.float32)
    wup = jnp.concatenate([w_up, zpad], axis=1)
    aup = jnp.concatenate([zpad, a_up], axis=1)
    chunk = lambda d, j: j + d * (nc - 1 - 2 * j)
    dsel3 = lambda bi, d, j: (d, 0, 0)
    const2 = lambda bi, d, j: (0, 0)
    nb = RWKV_SEQS
    return pl.pallas_call(
        _rwkv_kernel,
        grid=(b // nb, 2, nc),
        in_specs=[pl.BlockSpec((nb, CHUNK, D_A_COLS), lambda bi, d, j: (bi, chunk(d, j), 0)),
                  pl.BlockSpec((nb, 1, N_A, D_A), lambda bi, d, j: (bi, d, 0, 0)),
                  pl.BlockSpec((1, 1, D_A), dsel3),
                  pl.BlockSpec((1, LORA_W + LORA_A, D_A), dsel3),
                  pl.BlockSpec((1, 1, D_A), dsel3),
                  pl.BlockSpec((1, LORA_W + LORA_A, D_A), dsel3),
                  pl.BlockSpec((1, D_A), const2),
                  pl.BlockSpec((1, D_A), const2),
                  pl.BlockSpec((1, D_A), const2),
                  pl.BlockSpec((1, CHUNK, GROUP_LANES), dsel3),
                  pl.BlockSpec((1, CHUNK, GROUP_LANES), dsel3),
                  pl.BlockSpec((1, CHUNK, CHUNK), dsel3),
                  pl.BlockSpec((1, 6, CHUNK, GROUP_LANES), lambda bi, d, j: (d, 0, 0, 0)),
                  pl.BlockSpec((CHUNK, GROUP_LANES), const2),
                  pl.BlockSpec((GROUP_LANES, GROUP_LANES), const2)],
        out_specs=[pl.BlockSpec((1, nb, CHUNK, D_A), lambda bi, d, j: (d, bi, chunk(d, j), 0)),
                   pl.BlockSpec((1, nb, CHUNK, D_A), lambda bi, d, j: (d, bi, chunk(d, j), 0)),
                   pl.BlockSpec((nb, 1, N_A, D_A), lambda bi, d, j: (bi, d, 0, 0))],
        out_shape=[jax.ShapeDtypeStruct((2, b, t, D_A), jnp.float32),
                   jax.ShapeDtypeStruct((2, b, t, D_A), jnp.float32),
                   jax.ShapeDtypeStruct((b, 2, N_A, D_A), jnp.float32)],
        scratch_shapes=[pltpu.VMEM((nb, N_A, D_A), jnp.float32)],
        compiler_params=pltpu.CompilerParams(dimension_semantics=("arbitrary", "arbitrary", "arbitrary"),
                                             vmem_limit_bytes=VMEM_LIMIT),
        name="rwkv_scan",
    )(pa, s0, w0.reshape(2, 1, D_A), wup, a0.reshape(2, 1, D_A), aup,
      k_k.reshape(1, D_A), k_a.reshape(1, D_A), r_k.reshape(1, D_A),
      strict, incl, cmat, levels, eye, bd)


def _softmax_pv(scores, values):
    m = scores[0].max(axis=-1, keepdims=True)
    for s in scores[1:]:
        m = jnp.maximum(m, s.max(axis=-1, keepdims=True))
    den = None
    acc = None
    for s, val in zip(scores, values):
        p = jnp.exp(s - m)
        d = p.sum(axis=-1, keepdims=True)
        o = _dot(p.astype(_BF), val)
        den = d if den is None else den + d
        acc = o if acc is None else acc + o
    return acc / den


def _ctx_attn_kernel(qkv_ref, o_ref):
    for h in range(H_B):
        hs = slice(h * N_B, (h + 1) * N_B)
        q = (qkv_ref[0, :, hs] * ATTN_SCALE).astype(_BF)
        k = qkv_ref[0, :, D_B + h * N_B:D_B + (h + 1) * N_B].astype(_BF)
        v = qkv_ref[0, :, 2 * D_B + h * N_B:2 * D_B + (h + 1) * N_B].astype(_BF)
        o_ref[0, :, hs] = _softmax_pv([_dot(q, k, _NT)], [v])


def _context_attention(qkv):
    b, t, _ = qkv.shape
    return pl.pallas_call(
        _ctx_attn_kernel,
        grid=(b,),
        in_specs=[pl.BlockSpec((1, t, 3 * D_B), lambda i: (i, 0, 0))],
        out_specs=pl.BlockSpec((1, t, D_B), lambda i: (i, 0, 0)),
        out_shape=jax.ShapeDtypeStruct((b, t, D_B), jnp.float32),
        compiler_params=pltpu.CompilerParams(dimension_semantics=("arbitrary",),
                                             vmem_limit_bytes=VMEM_LIMIT),
        name="context_attention",
    )(qkv)


NA_BAND = WIN_R * GRID_W
NA_EDGE = WIN_R // 2


def _na_bias(rpb, rows):
    cfg_rows = list(range(NA_EDGE)) + [NA_EDGE] + list(range(rows - NA_EDGE, rows))
    r_ids = np.asarray(cfg_rows)
    row_start = np.clip(r_ids - WIN_R // 2, 0, rows - WIN_R)
    dr = row_start[:, None] + np.arange(WIN_R)[None, :] - r_ids[:, None] + WIN_R - 1
    c_ids = np.arange(GRID_W)
    col_start = np.clip(c_ids - WIN_C // 2, 0, GRID_W - WIN_C)
    valid = (c_ids[None, :] >= col_start[:, None]) & (c_ids[None, :] < col_start[:, None] + WIN_C)
    dc = np.clip(c_ids[None, :] - c_ids[:, None] + WIN_C - 1, 0, 2 * WIN_C - 2)
    rows_sel = jnp.stack([jnp.stack([rpb[:, int(d)] for d in dr_c], axis=1) for dr_c in dr], axis=1)
    onehot = jnp.asarray((dc.reshape(-1)[None, :] == np.arange(2 * WIN_C - 1)[:, None]).astype(np.float32))
    bias = jnp.dot(rows_sel, onehot, precision=lax.Precision.HIGHEST)
    bias = bias.reshape(H_B, len(cfg_rows), WIN_R, GRID_W, GRID_W)
    bias = jnp.where(valid[None, None, None, :, :], bias, NEG_INF)
    return jnp.transpose(bias, (1, 0, 3, 2, 4)).reshape(len(cfg_rows), H_B, GRID_W, NA_BAND)


def _na_kernel(q_ref, k_ref, v_ref, ck_ref, cv_ref, bias_ref, o_ref):
    r = pl.program_id(1)
    rows = k_ref.shape[1] // GRID_W
    start = pl.multiple_of(jnp.clip(r - WIN_R // 2, 0, rows - WIN_R) * GRID_W, GRID_W)
    for h in range(H_B):
        hs = slice(h * N_B, (h + 1) * N_B)
        q = q_ref[0, :, hs] * ATTN_SCALE
        k = k_ref[0, pl.ds(start, NA_BAND), hs]
        v = v_ref[0, pl.ds(start, NA_BAND), hs]
        s_loc = _dot(q, k, _NT) + bias_ref[0, h]
        s_ctx = _dot(q, ck_ref[0, :, hs], _NT)
        o_ref[0, :, hs] = _softmax_pv([s_loc, s_ctx], [v, cv_ref[0, :, hs]])


def _neighbourhood_attention(qkv_bf16, ctx_k, ctx_v, rpb):
    b, t, _ = qkv_bf16.shape
    rows = t // GRID_W
    past = ctx_k.shape[1]
    bias = _na_bias(rpb, rows)
    cfg = lambda bi, r: (jnp.minimum(r, NA_EDGE) + jnp.maximum(r - (rows - NA_EDGE - 1), 0), 0, 0, 0)
    return pl.pallas_call(
        _na_kernel,
        grid=(b, rows),
        in_specs=[pl.BlockSpec((1, GRID_W, D_B), lambda bi, r: (bi, r, 0)),
                  pl.BlockSpec((1, t, D_B), lambda bi, r: (bi, 0, 1)),
                  pl.BlockSpec((1, t, D_B), lambda bi, r: (bi, 0, 2)),
                  pl.BlockSpec((1, past, D_B), lambda bi, r: (bi, 0, 0)),
                  pl.BlockSpec((1, past, D_B), lambda bi, r: (bi, 0, 0)),
                  pl.BlockSpec((1, H_B, GRID_W, NA_BAND), cfg)],
        out_specs=pl.BlockSpec((1, GRID_W, D_B), lambda bi, r: (bi, r, 0)),
        out_shape=jax.ShapeDtypeStruct((b, t, D_B), jnp.float32),
        compiler_params=pltpu.CompilerParams(dimension_semantics=("arbitrary", "arbitrary"),
                                             vmem_limit_bytes=VMEM_LIMIT),
        name="neighbourhood_attention",
    )(qkv_bf16, qkv_bf16, qkv_bf16, ctx_k.reshape(b, past, D_B).astype(_BF),
      ctx_v.reshape(b, past, D_B).astype(_BF), bias)


def _outproj_kernel(y_ref, z_ref, glo_ref, yb_ref, x_ref, mod_ref, g2_ref, w_ref, router_ref,
                    gup_ref, gng_ref, gnb_ref, bd_ref, x1_ref, h2_ref, logit_ref):
    mod = mod_ref[0]
    bd = bd_ref[...]
    yf = y_ref[0] + y_ref[1]
    mu = _mmp(_pieces(yf, 3), [bd]) * (1.0 / N_A)
    dev = yf - mu
    var = _mmp(_pieces(dev * dev, 3), [bd]) * (1.0 / N_A)
    yn = dev * lax.rsqrt(var + GN_EPS)
    gate = _mmp(_pieces(jax.nn.sigmoid(glo_ref[...]), 2), _pieces(gup_ref[...], 2))
    ya = (yn * gng_ref[...] + gnb_ref[...] + z_ref[0] + z_ref[1]) * gate
    acc = _dot(ya.astype(_BF), w_ref[0:D_A, :]) + _dot(yb_ref[...].astype(_BF), w_ref[D_A:, :])
    x1 = x_ref[...] + mod[2:3] * acc
    x1_ref[...] = x1
    h2 = _rms_mod(x1, g2_ref[...], mod[4:5], mod[3:4])
    h2_ref[...] = h2.astype(_BF)
    logit_ref[...] = _mmp(_pieces(h2, 2), _pieces(router_ref[...], 2))


def _outproj(y, z, pa2d, yb, x2d, mod, mod_base, rows_per_mod, norm2_g, w_out_bf16, router_pad,
             g_up, gn_g, gn_b):
    n = x2d.shape[0]
    tm = ROW_TILE
    mod_idx = lambda i: (mod_base + (i * tm) // rows_per_mod, 0, 0)
    row = lambda i: (i, 0)
    row3 = lambda i: (0, i, 0)
    const = lambda i: (0, 0)
    bd = jnp.asarray(_head_block_mask(D_A).astype(np.float32)).astype(_BF)
    return pl.pallas_call(
        _outproj_kernel,
        grid=(n // tm,),
        in_specs=[pl.BlockSpec((2, tm, D_A), row3), pl.BlockSpec((2, tm, D_A), row3),
                  pl.BlockSpec((tm, LORA_G), lambda i: (i, (D_A_COLS - LORA_G) // LORA_G)),
                  pl.BlockSpec((tm, D_B), row),
                  pl.BlockSpec((tm, D_MODEL), row),
                  pl.BlockSpec((1, 6, D_MODEL), mod_idx),
                  pl.BlockSpec((1, D_MODEL), const),
                  pl.BlockSpec((D_MODEL, D_MODEL), const),
                  pl.BlockSpec((D_MODEL, 128), const),
                  pl.BlockSpec((LORA_G, D_A), const),
                  pl.BlockSpec((1, D_A), const),
                  pl.BlockSpec((1, D_A), const),
                  pl.BlockSpec((D_A, D_A), const)],
        out_specs=[pl.BlockSpec((tm, D_MODEL), row), pl.BlockSpec((tm, D_MODEL), row),
                   pl.BlockSpec((tm, 128), row)],
        out_shape=[jax.ShapeDtypeStruct((n, D_MODEL), jnp.float32),
                   jax.ShapeDtypeStruct((n, D_MODEL), _BF),
                   jax.ShapeDtypeStruct((n, 128), jnp.float32)],
        compiler_params=pltpu.CompilerParams(dimension_semantics=("arbitrary",),
                                             vmem_limit_bytes=VMEM_LIMIT),
        name="outproj",
    )(y, z, pa2d, yb, x2d, mod, norm2_g.reshape(1, D_MODEL), w_out_bf16, router_pad,
      g_up, gn_g.reshape(1, D_A), gn_b.reshape(1, D_A), bd)


MOE_TF = 256
MOE_TM = 512


def _moe_kernel(xe_ref, wg_ref, wu_ref, wd_ref, o_ref, wg_bf, wu_bf, wd_bf):
    f = pl.program_id(1)
    wg_bf[...] = wg_ref[0].astype(_BF)
    wu_bf[...] = wu_ref[0].astype(_BF)
    wd_bf[...] = wd_ref[0].astype(_BF)

    def rows(i, carry):
        sl = pl.ds(pl.multiple_of(i * MOE_TM, MOE_TM), MOE_TM)
        xe = xe_ref[0, sl, :]
        gt = _dot(xe, wg_bf[...])
        up = _dot(xe, wu_bf[...])
        hid = (gt * jax.nn.sigmoid(gt) * up).astype(_BF)
        part = _dot(hid, wd_bf[...])

        @pl.when(f == 0)
        def _():
            o_ref[0, sl, :] = part

        @pl.when(f > 0)
        def _():
            o_ref[0, sl, :] += part

        return carry

    lax.fori_loop(0, xe_ref.shape[1] // MOE_TM, rows, 0)


def _moe_experts(xe, e_gate, e_up, e_down):
    e, rows, _ = xe.shape
    nf = D_EXPERT // MOE_TF
    return pl.pallas_call(
        _moe_kernel,
        grid=(e, nf),
        in_specs=[pl.BlockSpec((1, rows, D_MODEL), lambda ei, f: (ei, 0, 0)),
                  pl.BlockSpec((1, D_MODEL, MOE_TF), lambda ei, f: (ei, 0, f)),
                  pl.BlockSpec((1, D_MODEL, MOE_TF), lambda ei, f: (ei, 0, f)),
                  pl.BlockSpec((1, MOE_TF, D_MODEL), lambda ei, f: (ei, f, 0))],
        out_specs=pl.BlockSpec((1, rows, D_MODEL), lambda ei, f: (ei, 0, 0)),
        out_shape=jax.ShapeDtypeStruct((e, rows, D_MODEL), jnp.float32),
        scratch_shapes=[pltpu.VMEM((D_MODEL, MOE_TF), _BF),
                        pltpu.VMEM((D_MODEL, MOE_TF), _BF),
                        pltpu.VMEM((MOE_TF, D_MODEL), _BF)],
        compiler_params=pltpu.CompilerParams(dimension_semantics=("arbitrary", "arbitrary"),
                                             vmem_limit_bytes=VMEM_LIMIT),
        name="moe_experts",
    )(xe, e_gate, e_up, e_down)


def _final_kernel(x1_ref, moe_ref, mod_ref, g_ref, o_ref):
    x2 = x1_ref[...] + mod_ref[0][5:6] * moe_ref[...]
    o_ref[...] = x2 * lax.rsqrt(jnp.mean(x2 * x2, axis=-1, keepdims=True) + RMS_EPS) * g_ref[...]


def _final(x1, moe, mod, mod_base, rows_per_mod, final_g):
    n = x1.shape[0]
    tm = ROW_TILE
    mod_idx = lambda i: (mod_base + (i * tm) // rows_per_mod, 0, 0)
    row = lambda i: (i, 0)
    return pl.pallas_call(
        _final_kernel,
        grid=(n // tm,),
        in_specs=[pl.BlockSpec((tm, D_MODEL), row), pl.BlockSpec((tm, D_MODEL), row),
                  pl.BlockSpec((1, 6, D_MODEL), mod_idx),
                  pl.BlockSpec((1, D_MODEL), lambda i: (0, 0))],
        out_specs=pl.BlockSpec((tm, D_MODEL), row),
        out_shape=jax.ShapeDtypeStruct((n, D_MODEL), jnp.float32),
        compiler_params=pltpu.CompilerParams(dimension_semantics=("arbitrary",),
                                             vmem_limit_bytes=VMEM_LIMIT),
        name="final_norm",
    )(x1, moe, mod, final_g.reshape(1, D_MODEL))


def _centred_shift(p, mu_prev, mu_next):
    zero = jnp.zeros_like(p[:, :1])
    p_prev = jnp.concatenate([zero, p[:, :-1]], axis=1)
    p_next = jnp.concatenate([p[:, 1:], zero], axis=1)
    return p + mu_prev * (p_prev - p) + mu_next * (p_next - p)


def _route(logits, b, t):
    cap = EC_FACTOR * t // N_EXPERTS
    aff = jax.nn.softmax(logits.reshape(b, t, N_EXPERTS), axis=-1)
    gate, idx = lax.top_k(jnp.swapaxes(aff, 1, 2), cap)
    return gate, idx


def kernel(x_prompt, x_sample, cache_na_k, cache_na_v, state_rwkv, c, c_ctx, final_norm_g, norm1_g, norm2_g,
           w_mod, b_mod, w_in, mu_prev, mu_next, w0, w_up, a0, a_up, g_up, k_k, k_a, r_k, gn_g, gn_b, rpb,
           w_out, router, e_gate, e_up, e_down):
    bp, tp, _ = x_prompt.shape
    bs, ts, _ = x_sample.shape
    l = 0
    cond8 = jnp.concatenate([c_ctx[None, :], c, jnp.zeros((8 - 1 - bs, D_MODEL), jnp.float32)], axis=0)
    mod = _adaln(cond8, w_mod[l], b_mod[l]).reshape(8, 6, D_MODEL)
    w_in_bf = w_in[l].astype(_BF)
    w_out_bf = w_out[l].astype(_BF)
    router_pad = jnp.pad(router[l], ((0, 0), (0, 128 - N_EXPERTS)))

    groups = []
    for name, x, mod_base, s0 in (
            ("prompt", x_prompt, 0, jnp.zeros((bp, 2, N_A, D_A), jnp.float32)),
            ("sample", x_sample, 1,
             jnp.transpose(state_rwkv[:, l], (0, 1, 3, 2, 4)).reshape(bs, 2, N_A, D_A))):
        b, t, _ = x.shape
        is_prompt = name == "prompt"
        rows_per_mod = b * t if is_prompt else t
        x2d = x.reshape(b * t, D_MODEL)
        pa, qkv = _inproj(x2d, mod, mod_base, rows_per_mod, norm1_g[l], w_in_bf,
                          jnp.float32 if is_prompt else _BF)
        pa = _centred_shift(pa.reshape(b, t, D_A_COLS), mu_prev[l], mu_next[l])
        qkv = qkv.reshape(b, t, 3 * D_B)
        y, z, s_fin = _rwkv_scan(pa, s0, w0[l], w_up[l], a0[l], a_up[l], k_k[l], k_a[l], r_k[l].reshape(D_A))
        if is_prompt:
            yb = _context_attention(qkv)
        else:
            yb = _neighbourhood_attention(qkv, cache_na_k[:, l], cache_na_v[:, l], rpb[l])
        x1, h2, logits = _outproj(y.reshape(2, b * t, D_A), z.reshape(2, b * t, D_A),
                                  pa.reshape(b * t, D_A_COLS), yb.reshape(b * t, D_B), x2d, mod, mod_base,
                                  rows_per_mod, norm2_g[l], w_out_bf, router_pad, g_up[l], gn_g[l], gn_b[l])
        gate, idx = _route(logits[:, :N_EXPERTS], b, t)
        groups.append(dict(b=b, t=t, x1=x1, h2=h2, gate=gate, idx=idx, qkv=qkv, s_fin=s_fin,
                           mod_base=mod_base, rows_per_mod=rows_per_mod))

    xes = []
    for gr in groups:
        b, t = gr["b"], gr["t"]
        h3 = gr["h2"].reshape(b, t, D_MODEL)
        xe = h3[jnp.arange(b)[:, None, None], gr["idx"]]
        xes.append(jnp.swapaxes(xe, 0, 1).reshape(N_EXPERTS, -1, D_MODEL))
    xe_all = jnp.concatenate(xes, axis=1)
    ye_all = _moe_experts(xe_all, e_gate[l], e_up[l], e_down[l])

    outs = []
    off = 0
    for gr in groups:
        b, t = gr["b"], gr["t"]
        cap = gr["idx"].shape[-1]
        ye = ye_all[:, off:off + b * cap].reshape(N_EXPERTS, b, cap, D_MODEL)
        off += b * cap
        ye = jnp.swapaxes(ye, 0, 1) * gr["gate"][..., None]
        moe = jnp.zeros((b, t, D_MODEL), jnp.float32).at[jnp.arange(b)[:, None, None], gr["idx"]].add(ye)
        y = _final(gr["x1"], moe.reshape(b * t, D_MODEL), mod, gr["mod_base"], gr["rows_per_mod"], final_norm_g)
        outs.append(y.reshape(b, t, D_MODEL))

    gp = groups[0]
    new_k = gp["qkv"][:, :, D_B:2 * D_B].reshape(bp, 1, tp, H_B, N_B)
    new_v = gp["qkv"][:, :, 2 * D_B:].reshape(bp, 1, tp, H_B, N_B)
    new_s = jnp.transpose(gp["s_fin"].reshape(bp, 2, N_A, H_A, N_A), (0, 1, 3, 2, 4))
    return (outs[0], outs[1], new_k, new_v, new_s[:, None])
```

```python
import numpy as np
import jax
import jax.numpy as jnp
from jax import lax
from jax.experimental import pallas as pl
from jax.experimental.pallas import tpu as pltpu

D_MODEL = 1024
GRID_W = 64
H_A = 8
N_A = 64
D_A = H_A * N_A
H_B = 8
N_B = 64
D_B = H_B * N_B
LORA_W = 64
LORA_A = 64
LORA_G = 128
D_A_COLS = 3 * D_A + LORA_W + LORA_A + LORA_G
D_IN = D_A_COLS + 3 * D_B
WIN_R = 8
WIN_C = 16
N_EXPERTS = 16
EC_FACTOR = 2
D_EXPERT = 2816
RMS_EPS = 1e-6
GN_EPS = 64e-5
ATTN_SCALE = N_B ** -0.5
NEG_INF = -1e30

CHUNK = 64
HEAD_GROUP = 4
GROUP_LANES = HEAD_GROUP * N_A
ROW_TILE = 512
VMEM_LIMIT = 56 * 1024 * 1024

_NN = (((1,), (0,)), ((), ()))
_NT = (((1,), (1,)), ((), ()))
_TN = (((0,), (0,)), ((), ()))
_BF = jnp.bfloat16


def _dot(a, b, dims=_NN):
    return lax.dot_general(a, b, dims, preferred_element_type=jnp.float32)


def _pieces(a, n):
    out = []
    for i in range(n):
        p = a.astype(_BF)
        out.append(p)
        if i + 1 < n:
            a = a - p.astype(jnp.float32)
    return out


def _mmp(ap, bp, dims=_NN):
    n = max(len(ap), len(bp))
    acc = None
    for i, x in enumerate(ap):
        for j, y in enumerate(bp):
            if i + j < n:
                d = _dot(x, y, dims)
                acc = d if acc is None else acc + d
    return acc


def _mod_kernel(c_ref, w_ref, b_ref, o_ref):
    c = c_ref[...]
    s = c * jax.nn.sigmoid(c)
    o_ref[...] = _dot(s.astype(_BF), w_ref[...].astype(_BF)) + b_ref[...]


def _adaln(cond8, w_mod, b_mod):
    n = w_mod.shape[1]
    tn = 1024
    return pl.pallas_call(
        _mod_kernel,
        grid=(n // tn,),
        in_specs=[pl.BlockSpec((8, D_MODEL), lambda j: (0, 0)),
                  pl.BlockSpec((D_MODEL, tn), lambda j: (0, j)),
                  pl.BlockSpec((1, tn), lambda j: (0, j))],
        out_specs=pl.BlockSpec((8, tn), lambda j: (0, j)),
        out_shape=jax.ShapeDtypeStruct((8, n), jnp.float32),
        compiler_params=pltpu.CompilerParams(dimension_semantics=("arbitrary",),
                                             vmem_limit_bytes=VMEM_LIMIT),
        name="adaln_mod",
    )(cond8, w_mod, b_mod.reshape(1, n))


def _rms_mod(x, g, scale, shift):
    y = x * lax.rsqrt(jnp.mean(x * x, axis=-1, keepdims=True) + RMS_EPS)
    return (y * g) * (1.0 + scale) + shift


def _inproj_kernel(x_ref, mod_ref, g_ref, w_ref, pa_ref, qkv_ref):
    mod = mod_ref[0]
    h = _rms_mod(x_ref[...], g_ref[...], mod[1:2], mod[0:1]).astype(_BF)
    for n0 in range(0, D_A_COLS, 256):
        pa_ref[:, n0:n0 + 256] = _dot(h, w_ref[:, n0:n0 + 256])
    for n0 in range(0, 3 * D_B, 256):
        qkv_ref[:, n0:n0 + 256] = _dot(h, w_ref[:, D_A_COLS + n0:D_A_COLS + n0 + 256]).astype(qkv_ref.dtype)


def _inproj(x2d, mod, mod_base, rows_per_mod, norm_g, w_in_bf16, qkv_dtype):
    n = x2d.shape[0]
    tm = ROW_TILE
    mod_idx = lambda i: (mod_base + (i * tm) // rows_per_mod, 0, 0)
    return pl.pallas_call(
        _inproj_kernel,
        grid=(n // tm,),
        in_specs=[pl.BlockSpec((tm, D_MODEL), lambda i: (i, 0)),
                  pl.BlockSpec((1, 6, D_MODEL), mod_idx),
                  pl.BlockSpec((1, D_MODEL), lambda i: (0, 0)),
                  pl.BlockSpec((D_MODEL, D_IN), lambda i: (0, 0))],
        out_specs=[pl.BlockSpec((tm, D_A_COLS), lambda i: (i, 0)),
                   pl.BlockSpec((tm, 3 * D_B), lambda i: (i, 0))],
        out_shape=[jax.ShapeDtypeStruct((n, D_A_COLS), jnp.float32),
                   jax.ShapeDtypeStruct((n, 3 * D_B), qkv_dtype)],
        compiler_params=pltpu.CompilerParams(dimension_semantics=("arbitrary",),
                                             vmem_limit_bytes=VMEM_LIMIT),
        name="inproj",
    )(x2d, mod, norm_g.reshape(1, D_MODEL), w_in_bf16)


RWKV_PIECES = 1
RWKV_SEQS = 2


def _head_block_mask(n):
    r = np.arange(n)
    return (r[:, None] // N_A) == (r[None, :] // N_A)


def _rwkv_masks():
    L = CHUNK
    i = np.arange(L)[:, None]
    j = np.arange(L)[None, :]
    strict, incl, cmat, levels = [], [], [], []
    for rev in (False, True):
        before = (j > i) if rev else (j < i)
        strict.append(np.tile(before, (1, HEAD_GROUP)))
        incl.append(np.tile(before | (i == j), (1, HEAD_GROUP)))
        cmat.append(before | (i == j))
        lv = []
        s = 1
        while s < L:
            same = (i // (2 * s)) == (j // (2 * s))
            late_i, early_j = (i % (2 * s)) >= s, (j % (2 * s)) < s
            m = same & late_i & early_j
            if rev:
                m = m.T
            lv.append(np.tile(m, (1, HEAD_GROUP)))
            s *= 2
        levels.append(np.stack(lv))
    f = lambda a: jnp.asarray(np.stack(a).astype(np.float32))
    eye = jnp.asarray(np.tile(np.eye(L, dtype=np.float32), (1, HEAD_GROUP)))
    bd = jnp.asarray(_head_block_mask(GROUP_LANES).astype(np.float32)).astype(_BF)
    return f(strict), f(incl), f(cmat).astype(_BF), f(levels), eye, bd


def _softplus(z):
    return jnp.maximum(z, 0.0) + jnp.log1p(jnp.exp(-jnp.abs(z)))


def _rwkv_kernel(pa_ref, s0_ref, w0_ref, wup_ref, a0_ref, aup_ref, kk_ref, ka_ref, rk_ref,
                 strict_ref, incl_ref, cmat_ref, lvl_ref, eye_ref, bd_ref,
                 y_ref, z_ref, sfin_ref, state_ref):
    j = pl.program_id(2)

    @pl.when(j == 0)
    def _():
        state_ref[...] = s0_ref[:, 0]

    L = CHUNK
    NP = RWKV_PIECES
    strict = strict_ref[0]
    incl = incl_ref[0]
    bd = bd_ref[...]
    eye = eye_ref[...]

    def expand(m, n=NP):
        return [jnp.concatenate([p] * HEAD_GROUP, axis=0) * bd for p in _pieces(m, n)]

    chains = []
    for bi in range(pa_ref.shape[0]):
        r = pa_ref[bi, :, 0:D_A]
        k = pa_ref[bi, :, D_A:2 * D_A]
        v = pa_ref[bi, :, 2 * D_A:3 * D_A]
        lo = pa_ref[bi, :, 3 * D_A:3 * D_A + LORA_W + LORA_A]
        wz = w0_ref[0] + _mmp(_pieces(jnp.tanh(lo), 2), _pieces(wup_ref[0], 2))
        lw = -jnp.exp(-_softplus(-wz) - 0.5)
        a = jax.nn.sigmoid(a0_ref[0] + _mmp(_pieces(lo, 2), _pieces(aup_ref[0], 2)))
        kd = k * (1.0 + (a - 1.0) * ka_ref[...])
        kkf = k * kk_ref[...]
        cum = _mmp([cmat_ref[0]], _pieces(lw, 3))
        tot = jnp.sum(lw, axis=0, keepdims=True)
        e_in = jnp.exp(cum)
        e_ex = jnp.exp(cum - lw)
        e_inv = jnp.exp(-cum)
        e_fin = jnp.exp(tot - cum)
        g_tot = jnp.exp(tot)
        for g in range(H_A // HEAD_GROUP):
            sl = slice(g * GROUP_LANES, (g + 1) * GROUP_LANES)
            kkf_g = kkf[:, sl]
            ss = _mmp(_pieces(kkf_g * kkf_g, 2), [bd])
            kk = kkf_g * lax.rsqrt(ss + 1e-12)
            bvec = kk * a[:, sl]
            r_g, kd_g, v_g = r[:, sl], kd[:, sl], v[:, sl]
            chains.append(dict(
                bi=bi, sl=sl, v=v_g, g_tot=g_tot[:, sl],
                x_ar=_pieces(jnp.concatenate([-kk * e_ex[:, sl], r_g * e_in[:, sl]], axis=0), NP),
                b_t=bvec * e_inv[:, sl], k_t=kd_g * e_inv[:, sl],
                bk_f=jnp.concatenate([bvec * e_fin[:, sl], kd_g * e_fin[:, sl]], axis=0),
                z=_mmp(_pieces(r_g * kd_g * rk_ref[:, sl], 2), [bd]) * v_g,
                s_old=state_ref[bi, :, sl]))

    for c in chains:
        c["p1"] = _mmp(c["x_ar"], expand(c["s_old"]), _NT)
        xb = _mmp(c["x_ar"], expand(c["b_t"]), _NT)
        xk = _mmp(c["x_ar"], expand(c["k_t"]), _NT)
        c["n_ab"] = xb[:L] * strict
        c["a_ak"] = xk[:L] * strict
        c["a_rb"] = xb[L:] * incl
        c["a_rk"] = xk[L:] * incl
        c["t"] = eye + c["n_ab"] * lvl_ref[0, 0]
    for lv in range(1, 6):
        for c in chains:
            c["tn"] = _mmp(_pieces(c["t"], NP), expand(c["n_ab"] * lvl_ref[0, lv]))
        for c in chains:
            c["t"] = c["t"] + _mmp(_pieces(c["tn"], NP), expand(c["t"]))
    for c in chains:
        c["bd_v"] = expand(c["v"])
        c["rhs"] = c["p1"][:L] + _mmp(_pieces(c["a_ak"], NP), c["bd_v"])
    for c in chains:
        c["u"] = _mmp(_pieces(c["t"], NP), expand(c["rhs"]))
    for c in chains:
        c["y"] = (c["p1"][L:] + _mmp(_pieces(c["a_rb"], NP), expand(c["u"]))
                  + _mmp(_pieces(c["a_rk"], NP), c["bd_v"]))
        q = _mmp(_pieces(jnp.concatenate([c["u"], c["v"]], axis=0), NP), _pieces(c["bk_f"], NP), _TN) * bd
        c["s_new"] = c["s_old"] * c["g_tot"] + (q[0:N_A] + q[N_A:2 * N_A] + q[2 * N_A:3 * N_A]
                                                + q[3 * N_A:4 * N_A])
    for c in chains:
        bi, sl = c["bi"], c["sl"]
        state_ref[bi, :, sl] = c["s_new"]
        sfin_ref[bi, 0, :, sl] = c["s_new"]
        y_ref[0, bi, :, sl] = c["y"]
        z_ref[0, bi, :, sl] = c["z"]


def _rwkv_scan(pa, s0, w0, w_up, a0, a_up, k_k, k_a, r_k):
    b, t, _ = pa.shape
    nc = t // CHUNK
    strict, incl, cmat, levels, eye, bd = _rwkv_masks()
    zpad = jnp.zeros((2, LORA_W, D_A), jnp.float32)
    wup = jnp.concatenate([w_up, zpad], axis=1)
    aup = jnp.concatenate([zpad, a_up], axis=1)
    chunk = lambda d, j: j + d * (nc - 1 - 2 * j)
    dsel3 = lambda bi, d, j: (d, 0, 0)
    const2 = lambda bi, d, j: (0, 0)
    nb = RWKV_SEQS
    return pl.pallas_call(
        _rwkv_kernel,
        grid=(b // nb, 2, nc),
        in_specs=[pl.BlockSpec((nb, CHUNK, D_A_COLS), lambda bi, d, j: (bi, chunk(d, j), 0)),
                  pl.BlockSpec((nb, 1, N_A, D_A), lambda bi, d, j: (bi, d, 0, 0)),
                  pl.BlockSpec((1, 1, D_A), dsel3),
                  pl.BlockSpec((1, LORA_W + LORA_A, D_A), dsel3),
                  pl.BlockSpec((1, 1, D_A), dsel3),
                  pl.BlockSpec((1, LORA_W + LORA_A, D_A), dsel3),
                  pl.BlockSpec((1, D_A), const2),
                  pl.BlockSpec((1, D_A), const2),
                  pl.BlockSpec((1, D_A), const2),
                  pl.BlockSpec((1, CHUNK, GROUP_LANES), dsel3),
                  pl.BlockSpec((1, CHUNK, GROUP_LANES), dsel3),
                  pl.BlockSpec((1, CHUNK, CHUNK), dsel3),
                  pl.BlockSpec((1, 6, CHUNK, GROUP_LANES), lambda bi, d, j: (d, 0, 0, 0)),
                  pl.BlockSpec((CHUNK, GROUP_LANES), const2),
                  pl.BlockSpec((GROUP_LANES, GROUP_LANES), const2)],
        out_specs=[pl.BlockSpec((1, nb, CHUNK, D_A), lambda bi, d, j: (d, bi, chunk(d, j), 0)),
                   pl.BlockSpec((1, nb, CHUNK, D_A), lambda bi, d, j: (d, bi, chunk(d, j), 0)),
                   pl.BlockSpec((nb, 1, N_A, D_A), lambda bi, d, j: (bi, d, 0, 0))],
        out_shape=[jax.ShapeDtypeStruct((2, b, t, D_A), jnp.float32),
                   jax.ShapeDtypeStruct((2, b, t, D_A), jnp.float32),
                   jax.ShapeDtypeStruct((b, 2, N_A, D_A), jnp.float32)],
        scratch_shapes=[pltpu.VMEM((nb, N_A, D_A), jnp.float32)],
        compiler_params=pltpu.CompilerParams(dimension_semantics=("arbitrary", "arbitrary", "arbitrary"),
                                             vmem_limit_bytes=VMEM_LIMIT),
        name="rwkv_scan",
    )(pa, s0, w0.reshape(2, 1, D_A), wup, a0.reshape(2, 1, D_A), aup,
      k_k.reshape(1, D_A), k_a.reshape(1, D_A), r_k.reshape(1, D_A),
      strict, incl, cmat, levels, eye, bd)


def _softmax_pv(scores, values):
    m = scores[0].max(axis=-1, keepdims=True)
    for s in scores[1:]:
        m = jnp.maximum(m, s.max(axis=-1, keepdims=True))
    den = None
    acc = None
    for s, val in zip(scores, values):
        p = jnp.exp(s - m)
        d = p.sum(axis=-1, keepdims=True)
        o = _dot(p.astype(_BF), val)
        den = d if den is None else den + d
        acc = o if acc is None else acc + o
    return acc / den


def _ctx_attn_kernel(qkv_ref, o_ref):
    for h in range(H_B):
        hs = slice(h * N_B, (h + 1) * N_B)
        q = (qkv_ref[0, :, hs] * ATTN_SCALE).astype(_BF)
        k = qkv_ref[0, :, D_B + h * N_B:D_B + (h + 1) * N_B].astype(_BF)
        v = qkv_ref[0, :, 2 * D_B + h * N_B:2 * D_B + (h + 1) * N_B].astype(_BF)
        o_ref[0, :, hs] = _softmax_pv([_dot(q, k, _NT)], [v])


def _context_attention(qkv):
    b, t, _ = qkv.shape
    return pl.pallas_call(
        _ctx_attn_kernel,
        grid=(b,),
        in_specs=[pl.BlockSpec((1, t, 3 * D_B), lambda i: (i, 0, 0))],
        out_specs=pl.BlockSpec((1, t, D_B), lambda i: (i, 0, 0)),
        out_shape=jax.ShapeDtypeStruct((b, t, D_B), jnp.float32),
        compiler_params=pltpu.CompilerParams(dimension_semantics=("arbitrary",),
                                             vmem_limit_bytes=VMEM_LIMIT),
        name="context_attention",
    )(qkv)


NA_BAND = WIN_R * GRID_W
NA_EDGE = WIN_R // 2


def _na_bias(rpb, rows):
    cfg_rows = list(range(NA_EDGE)) + [NA_EDGE] + list(range(rows - NA_EDGE, rows))
    r_ids = np.asarray(cfg_rows)
    row_start = np.clip(r_ids - WIN_R // 2, 0, rows - WIN_R)
    dr = row_start[:, None] + np.arange(WIN_R)[None, :] - r_ids[:, None] + WIN_R - 1
    c_ids = np.arange(GRID_W)
    col_start = np.clip(c_ids - WIN_C // 2, 0, GRID_W - WIN_C)
    valid = (c_ids[None, :] >= col_start[:, None]) & (c_ids[None, :] < col_start[:, None] + WIN_C)
    dc = np.clip(c_ids[None, :] - c_ids[:, None] + WIN_C - 1, 0, 2 * WIN_C - 2)
    rows_sel = jnp.stack([jnp.stack([rpb[:, int(d)] for d in dr_c], axis=1) for dr_c in dr], axis=1)
    onehot = jnp.asarray((dc.reshape(-1)[None, :] == np.arange(2 * WIN_C - 1)[:, None]).astype(np.float32))
    bias = jnp.dot(rows_sel, onehot, precision=lax.Precision.HIGHEST)
    bias = bias.reshape(H_B, len(cfg_rows), WIN_R, GRID_W, GRID_W)
    bias = jnp.where(valid[None, None, None, :, :], bias, NEG_INF)
    return jnp.transpose(bias, (1, 0, 3, 2, 4)).reshape(len(cfg_rows), H_B, GRID_W, NA_BAND)


def _na_kernel(q_ref, k_ref, v_ref, ck_ref, cv_ref, bias_ref, o_ref):
    r = pl.program_id(1)
    rows = k_ref.shape[1] // GRID_W
    start = pl.multiple_of(jnp.clip(r - WIN_R // 2, 0, rows - WIN_R) * GRID_W, GRID_W)
    for h in range(H_B):
        hs = slice(h * N_B, (h + 1) * N_B)
        q = q_ref[0, :, hs] * ATTN_SCALE
        k = k_ref[0, pl.ds(start, NA_BAND), hs]
        v = v_ref[0, pl.ds(start, NA_BAND), hs]
        s_loc = _dot(q, k, _NT) + bias_ref[0, h]
        s_ctx = _dot(q, ck_ref[0, :, hs], _NT)
        o_ref[0, :, hs] = _softmax_pv([s_loc, s_ctx], [v, cv_ref[0, :, hs]])


def _neighbourhood_attention(qkv_bf16, ctx_k, ctx_v, rpb):
    b, t, _ = qkv_bf16.shape
    rows = t // GRID_W
    past = ctx_k.shape[1]
    bias = _na_bias(rpb, rows)
    cfg = lambda bi, r: (jnp.minimum(r, NA_EDGE) + jnp.maximum(r - (rows - NA_EDGE - 1), 0), 0, 0, 0)
    return pl.pallas_call(
        _na_kernel,
        grid=(b, rows),
        in_specs=[pl.BlockSpec((1, GRID_W, D_B), lambda bi, r: (bi, r, 0)),
                  pl.BlockSpec((1, t, D_B), lambda bi, r: (bi, 0, 1)),
                  pl.BlockSpec((1, t, D_B), lambda bi, r: (bi, 0, 2)),
                  pl.BlockSpec((1, past, D_B), lambda bi, r: (bi, 0, 0)),
                  pl.BlockSpec((1, past, D_B), lambda bi, r: (bi, 0, 0)),
                  pl.BlockSpec((1, H_B, GRID_W, NA_BAND), cfg)],
        out_specs=pl.BlockSpec((1, GRID_W, D_B), lambda bi, r: (bi, r, 0)),
        out_shape=jax.ShapeDtypeStruct((b, t, D_B), jnp.float32),
        compiler_params=pltpu.CompilerParams(dimension_semantics=("arbitrary", "arbitrary"),
                                             vmem_limit_bytes=VMEM_LIMIT),
        name="neighbourhood_attention",
    )(qkv_bf16, qkv_bf16, qkv_bf16, ctx_k.reshape(b, past, D_B).astype(_BF),
      ctx_v.reshape(b, past, D_B).astype(_BF), bias)


def _outproj_kernel(y_ref, z_ref, glo_ref, yb_ref, x_ref, mod_ref, g2_ref, w_ref, router_ref,
                    gup_ref, gng_ref, gnb_ref, bd_ref, x1_ref, h2_ref, logit_ref):
    mod = mod_ref[0]
    bd = bd_ref[...]
    yf = y_ref[0] + y_ref[1]
    mu = _mmp(_pieces(yf, 3), [bd]) * (1.0 / N_A)
    dev = yf - mu
    var = _mmp(_pieces(dev * dev, 3), [bd]) * (1.0 / N_A)
    yn = dev * lax.rsqrt(var + GN_EPS)
    gate = _mmp(_pieces(jax.nn.sigmoid(glo_ref[...]), 2), _pieces(gup_ref[...], 2))
    ya = (yn * gng_ref[...] + gnb_ref[...] + z_ref[0] + z_ref[1]) * gate
    acc = _dot(ya.astype(_BF), w_ref[0:D_A, :]) + _dot(yb_ref[...].astype(_BF), w_ref[D_A:, :])
    x1 = x_ref[...] + mod[2:3] * acc
    x1_ref[...] = x1
    h2 = _rms_mod(x1, g2_ref[...], mod[4:5], mod[3:4])
    h2_ref[...] = h2.astype(_BF)
    logit_ref[...] = _mmp(_pieces(h2, 2), _pieces(router_ref[...], 2))


def _outproj(y, z, pa2d, yb, x2d, mod, mod_base, rows_per_mod, norm2_g, w_out_bf16, router_pad,
             g_up, gn_g, gn_b):
    n = x2d.shape[0]
    tm = ROW_TILE
    mod_idx = lambda i: (mod_base + (i * tm) // rows_per_mod, 0, 0)
    row = lambda i: (i, 0)
    row3 = lambda i: (0, i, 0)
    const = lambda i: (0, 0)
    bd = jnp.asarray(_head_block_mask(D_A).astype(np.float32)).astype(_BF)
    return pl.pallas_call(
        _outproj_kernel,
        grid=(n // tm,),
        in_specs=[pl.BlockSpec((2, tm, D_A), row3), pl.BlockSpec((2, tm, D_A), row3),
                  pl.BlockSpec((tm, LORA_G), lambda i: (i, (D_A_COLS - LORA_G) // LORA_G)),
                  pl.BlockSpec((tm, D_B), row),
                  pl.BlockSpec((tm, D_MODEL), row),
                  pl.BlockSpec((1, 6, D_MODEL), mod_idx),
                  pl.BlockSpec((1, D_MODEL), const),
                  pl.BlockSpec((D_MODEL, D_MODEL), const),
                  pl.BlockSpec((D_MODEL, 128), const),
                  pl.BlockSpec((LORA_G, D_A), const),
                  pl.BlockSpec((1, D_A), const),
                  pl.BlockSpec((1, D_A), const),
                  pl.BlockSpec((D_A, D_A), const)],
        out_specs=[pl.BlockSpec((tm, D_MODEL), row), pl.BlockSpec((tm, D_MODEL), row),
                   pl.BlockSpec((tm, 128), row)],
        out_shape=[jax.ShapeDtypeStruct((n, D_MODEL), jnp.float32),
                   jax.ShapeDtypeStruct((n, D_MODEL), _BF),
                   jax.ShapeDtypeStruct((n, 128), jnp.float32)],
        compiler_params=pltpu.CompilerParams(dimension_semantics=("arbitrary",),
                                             vmem_limit_bytes=VMEM_LIMIT),
        name="outproj",
    )(y, z, pa2d, yb, x2d, mod, norm2_g.reshape(1, D_MODEL), w_out_bf16, router_pad,
      g_up, gn_g.reshape(1, D_A), gn_b.reshape(1, D_A), bd)


MOE_TF = 256
MOE_TM = 512


def _moe_kernel(xp_ref, xs_ref, wg_ref, wu_ref, wd_ref, op_ref, os_ref, wg_bf, wu_bf, wd_bf):
    f = pl.program_id(1)
    wg_bf[...] = wg_ref[0].astype(_BF)
    wu_bf[...] = wu_ref[0].astype(_BF)
    wd_bf[...] = wd_ref[0].astype(_BF)

    for x_ref, o_ref in ((xp_ref, op_ref), (xs_ref, os_ref)):
        def rows(i, carry, x_ref=x_ref, o_ref=o_ref):
            sl = pl.ds(pl.multiple_of(i * MOE_TM, MOE_TM), MOE_TM)
            xe = x_ref[0, sl, :]
            gt = _dot(xe, wg_bf[...])
            up = _dot(xe, wu_bf[...])
            hid = (gt * jax.nn.sigmoid(gt) * up).astype(_BF)
            part = _dot(hid, wd_bf[...])

            @pl.when(f == 0)
            def _():
                o_ref[0, sl, :] = part

            @pl.when(f > 0)
            def _():
                o_ref[0, sl, :] += part

            return carry

        lax.fori_loop(0, x_ref.shape[1] // MOE_TM, rows, 0)


def _moe_experts(xe_p, xe_s, e_gate, e_up, e_down):
    e = xe_p.shape[0]
    nf = D_EXPERT // MOE_TF
    slab = lambda x: pl.BlockSpec((1, x.shape[1], D_MODEL), lambda ei, f: (ei, 0, 0))
    return pl.pallas_call(
        _moe_kernel,
        grid=(e, nf),
        in_specs=[slab(xe_p), slab(xe_s),
                  pl.BlockSpec((1, D_MODEL, MOE_TF), lambda ei, f: (ei, 0, f)),
                  pl.BlockSpec((1, D_MODEL, MOE_TF), lambda ei, f: (ei, 0, f)),
                  pl.BlockSpec((1, MOE_TF, D_MODEL), lambda ei, f: (ei, f, 0))],
        out_specs=[slab(xe_p), slab(xe_s)],
        out_shape=[jax.ShapeDtypeStruct(xe_p.shape, jnp.float32),
                   jax.ShapeDtypeStruct(xe_s.shape, jnp.float32)],
        scratch_shapes=[pltpu.VMEM((D_MODEL, MOE_TF), _BF),
                        pltpu.VMEM((D_MODEL, MOE_TF), _BF),
                        pltpu.VMEM((MOE_TF, D_MODEL), _BF)],
        compiler_params=pltpu.CompilerParams(dimension_semantics=("arbitrary", "arbitrary"),
                                             vmem_limit_bytes=VMEM_LIMIT),
        name="moe_experts",
    )(xe_p, xe_s, e_gate, e_up, e_down)


MOE_SLOTS = 512


def _gather_kernel(idx_ref, h_ref, o_ref):
    t = h_ref.shape[1]
    eb, cap, _ = o_ref.shape
    sel = (lax.broadcasted_iota(jnp.int32, (MOE_SLOTS, t), 1) == idx_ref[0]).astype(_BF)
    rows = _dot(sel, h_ref[0]).astype(_BF)
    for e in range(eb):
        o_ref[e] = rows[e * cap:(e + 1) * cap]


def _moe_gather(h, idx):
    b, t, _ = h.shape
    _, e, cap = idx.shape
    eb = MOE_SLOTS // cap
    return pl.pallas_call(
        _gather_kernel,
        grid=(b, e // eb),
        in_specs=[pl.BlockSpec((1, MOE_SLOTS, 1), lambda bi, eg: (bi, eg, 0)),
                  pl.BlockSpec((1, t, D_MODEL), lambda bi, eg: (bi, 0, 0))],
        out_specs=pl.BlockSpec((eb, cap, D_MODEL), lambda bi, eg: (eg, bi, 0)),
        out_shape=jax.ShapeDtypeStruct((e, b * cap, D_MODEL), _BF),
        compiler_params=pltpu.CompilerParams(dimension_semantics=("arbitrary", "arbitrary"),
                                             vmem_limit_bytes=VMEM_LIMIT),
        name="moe_gather",
    )(idx.reshape(b, e * cap, 1), h)


def _scatter_kernel(idx_ref, gate_ref, ye_ref, o_ref):
    eg = pl.program_id(1)
    t = o_ref.shape[1]
    eb = ye_ref.shape[0]
    ye = jnp.concatenate([ye_ref[e] for e in range(eb)], axis=0) * gate_ref[0]
    ye = ye.astype(_BF)
    tt = min(t, ROW_TILE)
    for t0 in range(0, t, tt):
        sel = (lax.broadcasted_iota(jnp.int32, (tt, MOE_SLOTS), 0) + t0 == idx_ref[0]).astype(_BF)
        part = _dot(sel, ye)

        @pl.when(eg == 0)
        def _():
            o_ref[0, t0:t0 + tt, :] = part

        @pl.when(eg > 0)
        def _():
            o_ref[0, t0:t0 + tt, :] += part


def _moe_scatter(ye, idx, gate, t):
    b, e, cap = idx.shape
    eb = MOE_SLOTS // cap
    return pl.pallas_call(
        _scatter_kernel,
        grid=(b, e // eb),
        in_specs=[pl.BlockSpec((1, 1, MOE_SLOTS), lambda bi, eg: (bi, 0, eg)),
                  pl.BlockSpec((1, MOE_SLOTS, 1), lambda bi, eg: (bi, eg, 0)),
                  pl.BlockSpec((eb, cap, D_MODEL), lambda bi, eg: (eg, bi, 0))],
        out_specs=pl.BlockSpec((1, t, D_MODEL), lambda bi, eg: (bi, 0, 0)),
        out_shape=jax.ShapeDtypeStruct((b, t, D_MODEL), jnp.float32),
        compiler_params=pltpu.CompilerParams(dimension_semantics=("arbitrary", "arbitrary"),
                                             vmem_limit_bytes=VMEM_LIMIT),
        name="moe_scatter",
    )(idx.reshape(b, 1, e * cap), gate.reshape(b, e * cap, 1), ye)


def _final_kernel(x1_ref, moe_ref, mod_ref, g_ref, o_ref):
    x2 = x1_ref[...] + mod_ref[0][5:6] * moe_ref[...]
    o_ref[...] = x2 * lax.rsqrt(jnp.mean(x2 * x2, axis=-1, keepdims=True) + RMS_EPS) * g_ref[...]


def _final(x1, moe, mod, mod_base, rows_per_mod, final_g):
    n = x1.shape[0]
    tm = ROW_TILE
    mod_idx = lambda i: (mod_base + (i * tm) // rows_per_mod, 0, 0)
    row = lambda i: (i, 0)
    return pl.pallas_call(
        _final_kernel,
        grid=(n // tm,),
        in_specs=[pl.BlockSpec((tm, D_MODEL), row), pl.BlockSpec((tm, D_MODEL), row),
                  pl.BlockSpec((1, 6, D_MODEL), mod_idx),
                  pl.BlockSpec((1, D_MODEL), lambda i: (0, 0))],
        out_specs=pl.BlockSpec((tm, D_MODEL), row),
        out_shape=jax.ShapeDtypeStruct((n, D_MODEL), jnp.float32),
        compiler_params=pltpu.CompilerParams(dimension_semantics=("arbitrary",),
                                             vmem_limit_bytes=VMEM_LIMIT),
        name="final_norm",
    )(x1, moe, mod, final_g.reshape(1, D_MODEL))


def _centred_shift(p, mu_prev, mu_next):
    zero = jnp.zeros_like(p[:, :1])
    p_prev = jnp.concatenate([zero, p[:, :-1]], axis=1)
    p_next = jnp.concatenate([p[:, 1:], zero], axis=1)
    return p + mu_prev * (p_prev - p) + mu_next * (p_next - p)


def _route(logits, b, t):
    cap = EC_FACTOR * t // N_EXPERTS
    aff = jax.nn.softmax(logits.reshape(b, t, N_EXPERTS), axis=-1)
    gate, idx = lax.top_k(jnp.swapaxes(aff, 1, 2), cap)
    return gate, idx


def kernel(x_prompt, x_sample, cache_na_k, cache_na_v, state_rwkv, c, c_ctx, final_norm_g, norm1_g, norm2_g,
           w_mod, b_mod, w_in, mu_prev, mu_next, w0, w_up, a0, a_up, g_up, k_k, k_a, r_k, gn_g, gn_b, rpb,
           w_out, router, e_gate, e_up, e_down):
    bp, tp, _ = x_prompt.shape
    bs, ts, _ = x_sample.shape
    l = 0
    cond8 = jnp.concatenate([c_ctx[None, :], c, jnp.zeros((8 - 1 - bs, D_MODEL), jnp.float32)], axis=0)
    mod = _adaln(cond8, w_mod[l], b_mod[l]).reshape(8, 6, D_MODEL)
    w_in_bf = w_in[l].astype(_BF)
    w_out_bf = w_out[l].astype(_BF)
    router_pad = jnp.pad(router[l], ((0, 0), (0, 128 - N_EXPERTS)))

    groups = []
    for name, x, mod_base, s0 in (
            ("prompt", x_prompt, 0, jnp.zeros((bp, 2, N_A, D_A), jnp.float32)),
            ("sample", x_sample, 1,
             jnp.transpose(state_rwkv[:, l], (0, 1, 3, 2, 4)).reshape(bs, 2, N_A, D_A))):
        b, t, _ = x.shape
        is_prompt = name == "prompt"
        rows_per_mod = b * t if is_prompt else t
        x2d = x.reshape(b * t, D_MODEL)
        pa, qkv = _inproj(x2d, mod, mod_base, rows_per_mod, norm1_g[l], w_in_bf,
                          jnp.float32 if is_prompt else _BF)
        pa = _centred_shift(pa.reshape(b, t, D_A_COLS), mu_prev[l], mu_next[l])
        qkv = qkv.reshape(b, t, 3 * D_B)
        y, z, s_fin = _rwkv_scan(pa, s0, w0[l], w_up[l], a0[l], a_up[l], k_k[l], k_a[l], r_k[l].reshape(D_A))
        if is_prompt:
            yb = _context_attention(qkv)
        else:
            yb = _neighbourhood_attention(qkv, cache_na_k[:, l], cache_na_v[:, l], rpb[l])
        x1, h2, logits = _outproj(y.reshape(2, b * t, D_A), z.reshape(2, b * t, D_A),
                                  pa.reshape(b * t, D_A_COLS), yb.reshape(b * t, D_B), x2d, mod, mod_base,
                                  rows_per_mod, norm2_g[l], w_out_bf, router_pad, g_up[l], gn_g[l], gn_b[l])
        gate, idx = _route(logits[:, :N_EXPERTS], b, t)
        groups.append(dict(b=b, t=t, x1=x1, h2=h2, gate=gate, idx=idx, qkv=qkv, s_fin=s_fin,
                           mod_base=mod_base, rows_per_mod=rows_per_mod))

    xes = [_moe_gather(gr["h2"].reshape(gr["b"], gr["t"], D_MODEL), gr["idx"]) for gr in groups]
    yes = _moe_experts(xes[0], xes[1], e_gate[l], e_up[l], e_down[l])

    outs = []
    for gr, ye in zip(groups, yes):
        b, t = gr["b"], gr["t"]
        moe = _moe_scatter(ye, gr["idx"], gr["gate"], t)
        y = _final(gr["x1"], moe.reshape(b * t, D_MODEL), mod, gr["mod_base"], gr["rows_per_mod"], final_norm_g)
        outs.append(y.reshape(b, t, D_MODEL))

    gp = groups[0]
    new_k = gp["qkv"][:, :, D_B:2 * D_B].reshape(bp, 1, tp, H_B, N_B)
    new_v = gp["qkv"][:, :, 2 * D_B:].reshape(bp, 1, tp, H_B, N_B)
    new_s = jnp.transpose(gp["s_fin"].reshape(bp, 2, N_A, H_A, N_A), (0, 1, 3, 2, 4))
    return (outs[0], outs[1], new_k, new_v, new_s[:, None])
```

```python
import numpy as np
import jax
import jax.numpy as jnp
from jax import lax
from jax.experimental import pallas as pl
from jax.experimental.pallas import tpu as pltpu

D_MODEL = 1024
GRID_W = 64
H_A = 8
N_A = 64
D_A = H_A * N_A
H_B = 8
N_B = 64
D_B = H_B * N_B
LORA_W = 64
LORA_A = 64
LORA_G = 128
D_A_COLS = 3 * D_A + LORA_W + LORA_A + LORA_G
D_IN = D_A_COLS + 3 * D_B
WIN_R = 8
WIN_C = 16
N_EXPERTS = 16
EC_FACTOR = 2
D_EXPERT = 2816
RMS_EPS = 1e-6
GN_EPS = 64e-5
ATTN_SCALE = N_B ** -0.5
NEG_INF = -1e30

CHUNK = 64
HEAD_GROUP = 4
GROUP_LANES = HEAD_GROUP * N_A
ROW_TILE = 512
VMEM_LIMIT = 56 * 1024 * 1024

_NN = (((1,), (0,)), ((), ()))
_NT = (((1,), (1,)), ((), ()))
_TN = (((0,), (0,)), ((), ()))
_BF = jnp.bfloat16


def _dot(a, b, dims=_NN):
    return lax.dot_general(a, b, dims, preferred_element_type=jnp.float32)


def _pieces(a, n):
    out = []
    for i in range(n):
        p = a.astype(_BF)
        out.append(p)
        if i + 1 < n:
            a = a - p.astype(jnp.float32)
    return out


def _mmp(ap, bp, dims=_NN):
    n = max(len(ap), len(bp))
    acc = None
    for i, x in enumerate(ap):
        for j, y in enumerate(bp):
            if i + j < n:
                d = _dot(x, y, dims)
                acc = d if acc is None else acc + d
    return acc


def _mod_kernel(c_ref, w_ref, b_ref, o_ref):
    c = c_ref[...]
    s = c * jax.nn.sigmoid(c)
    o_ref[...] = _dot(s.astype(_BF), w_ref[...].astype(_BF)) + b_ref[...]


def _adaln(cond8, w_mod, b_mod):
    n = w_mod.shape[1]
    tn = 1024
    return pl.pallas_call(
        _mod_kernel,
        grid=(n // tn,),
        in_specs=[pl.BlockSpec((8, D_MODEL), lambda j: (0, 0)),
                  pl.BlockSpec((D_MODEL, tn), lambda j: (0, j)),
                  pl.BlockSpec((1, tn), lambda j: (0, j))],
        out_specs=pl.BlockSpec((8, tn), lambda j: (0, j)),
        out_shape=jax.ShapeDtypeStruct((8, n), jnp.float32),
        compiler_params=pltpu.CompilerParams(dimension_semantics=("arbitrary",),
                                             vmem_limit_bytes=VMEM_LIMIT),
        name="adaln_mod",
    )(cond8, w_mod, b_mod.reshape(1, n))


def _rms_mod(x, g, scale, shift):
    y = x * lax.rsqrt(jnp.mean(x * x, axis=-1, keepdims=True) + RMS_EPS)
    return (y * g) * (1.0 + scale) + shift


def _inproj_kernel(x_ref, mod_ref, g_ref, w_ref, pa_ref, qkv_ref):
    mod = mod_ref[0]
    h = _rms_mod(x_ref[...], g_ref[...], mod[1:2], mod[0:1]).astype(_BF)
    for n0 in range(0, D_A_COLS, 256):
        pa_ref[:, n0:n0 + 256] = _dot(h, w_ref[:, n0:n0 + 256])
    for n0 in range(0, 3 * D_B, 256):
        qkv_ref[:, n0:n0 + 256] = _dot(h, w_ref[:, D_A_COLS + n0:D_A_COLS + n0 + 256]).astype(qkv_ref.dtype)


def _inproj(x2d, mod, mod_base, rows_per_mod, norm_g, w_in_bf16, qkv_dtype):
    n = x2d.shape[0]
    tm = ROW_TILE
    mod_idx = lambda i: (mod_base + (i * tm) // rows_per_mod, 0, 0)
    return pl.pallas_call(
        _inproj_kernel,
        grid=(n // tm,),
        in_specs=[pl.BlockSpec((tm, D_MODEL), lambda i: (i, 0)),
                  pl.BlockSpec((1, 6, D_MODEL), mod_idx),
                  pl.BlockSpec((1, D_MODEL), lambda i: (0, 0)),
                  pl.BlockSpec((D_MODEL, D_IN), lambda i: (0, 0))],
        out_specs=[pl.BlockSpec((tm, D_A_COLS), lambda i: (i, 0)),
                   pl.BlockSpec((tm, 3 * D_B), lambda i: (i, 0))],
        out_shape=[jax.ShapeDtypeStruct((n, D_A_COLS), jnp.float32),
                   jax.ShapeDtypeStruct((n, 3 * D_B), qkv_dtype)],
        compiler_params=pltpu.CompilerParams(dimension_semantics=("arbitrary",),
                                             vmem_limit_bytes=VMEM_LIMIT),
        name="inproj",
    )(x2d, mod, norm_g.reshape(1, D_MODEL), w_in_bf16)


RWKV_PIECES = 1
RWKV_SEQS = 4


def _head_block_mask(n):
    r = np.arange(n)
    return (r[:, None] // N_A) == (r[None, :] // N_A)


def _rwkv_masks():
    L = CHUNK
    i = np.arange(L)[:, None]
    j = np.arange(L)[None, :]
    strict, incl, cmat, levels = [], [], [], []
    for rev in (False, True):
        before = (j > i) if rev else (j < i)
        strict.append(np.tile(before, (1, HEAD_GROUP)))
        incl.append(np.tile(before | (i == j), (1, HEAD_GROUP)))
        cmat.append(before | (i == j))
        lv = []
        s = 1
        while s < L:
            same = (i // (2 * s)) == (j // (2 * s))
            late_i, early_j = (i % (2 * s)) >= s, (j % (2 * s)) < s
            m = same & late_i & early_j
            if rev:
                m = m.T
            lv.append(np.tile(m, (1, HEAD_GROUP)))
            s *= 2
        levels.append(np.stack(lv))
    f = lambda a: jnp.asarray(np.stack(a).astype(np.float32))
    eye = jnp.asarray(np.tile(np.eye(L, dtype=np.float32), (1, HEAD_GROUP)))
    bd = jnp.asarray(_head_block_mask(GROUP_LANES).astype(np.float32)).astype(_BF)
    return f(strict), f(incl), f(cmat).astype(_BF), f(levels), eye, bd


def _softplus(z):
    return jnp.maximum(z, 0.0) + jnp.log1p(jnp.exp(-jnp.abs(z)))


def _rwkv_kernel(pa_ref, s0_ref, w0_ref, wup_ref, a0_ref, aup_ref, kk_ref, ka_ref, rk_ref,
                 strict_ref, incl_ref, cmat_ref, lvl_ref, eye_ref, bd_ref,
                 y_ref, z_ref, sfin_ref, state_ref):
    j = pl.program_id(2)

    @pl.when(j == 0)
    def _():
        state_ref[...] = s0_ref[:, 0]

    L = CHUNK
    NP = RWKV_PIECES
    strict = strict_ref[0]
    incl = incl_ref[0]
    bd = bd_ref[...]
    eye = eye_ref[...]

    def expand(m, n=NP):
        return [jnp.concatenate([p] * HEAD_GROUP, axis=0) * bd for p in _pieces(m, n)]

    chains = []
    for bi in range(pa_ref.shape[0]):
        r = pa_ref[bi, :, 0:D_A]
        k = pa_ref[bi, :, D_A:2 * D_A]
        v = pa_ref[bi, :, 2 * D_A:3 * D_A]
        lo = pa_ref[bi, :, 3 * D_A:3 * D_A + LORA_W + LORA_A]
        wz = w0_ref[0] + _mmp(_pieces(jnp.tanh(lo), 2), _pieces(wup_ref[0], 2))
        lw = -jnp.exp(-_softplus(-wz) - 0.5)
        a = jax.nn.sigmoid(a0_ref[0] + _mmp(_pieces(lo, 2), _pieces(aup_ref[0], 2)))
        kd = k * (1.0 + (a - 1.0) * ka_ref[...])
        kkf = k * kk_ref[...]
        cum = _mmp([cmat_ref[0]], _pieces(lw, 3))
        tot = jnp.sum(lw, axis=0, keepdims=True)
        e_in = jnp.exp(cum)
        e_ex = jnp.exp(cum - lw)
        e_inv = jnp.exp(-cum)
        e_fin = jnp.exp(tot - cum)
        g_tot = jnp.exp(tot)
        for g in range(H_A // HEAD_GROUP):
            sl = slice(g * GROUP_LANES, (g + 1) * GROUP_LANES)
            kkf_g = kkf[:, sl]
            ss = _mmp(_pieces(kkf_g * kkf_g, 2), [bd])
            kk = kkf_g * lax.rsqrt(ss + 1e-12)
            bvec = kk * a[:, sl]
            r_g, kd_g, v_g = r[:, sl], kd[:, sl], v[:, sl]
            chains.append(dict(
                bi=bi, sl=sl, v=v_g, g_tot=g_tot[:, sl],
                x_ar=_pieces(jnp.concatenate([-kk * e_ex[:, sl], r_g * e_in[:, sl]], axis=0), NP),
                b_t=bvec * e_inv[:, sl], k_t=kd_g * e_inv[:, sl],
                bk_f=jnp.concatenate([bvec * e_fin[:, sl], kd_g * e_fin[:, sl]], axis=0),
                z=_mmp(_pieces(r_g * kd_g * rk_ref[:, sl], 2), [bd]) * v_g,
                s_old=state_ref[bi, :, sl]))

    for c in chains:
        c["p1"] = _mmp(c["x_ar"], expand(c["s_old"]), _NT)
        xb = _mmp(c["x_ar"], expand(c["b_t"]), _NT)
        xk = _mmp(c["x_ar"], expand(c["k_t"]), _NT)
        c["n_ab"] = xb[:L] * strict
        c["a_ak"] = xk[:L] * strict
        c["a_rb"] = xb[L:] * incl
        c["a_rk"] = xk[L:] * incl
        c["t"] = eye + c["n_ab"] * lvl_ref[0, 0]
    for lv in range(1, 6):
        for c in chains:
            c["tn"] = _mmp(_pieces(c["t"], NP), expand(c["n_ab"] * lvl_ref[0, lv]))
        for c in chains:
            c["t"] = c["t"] + _mmp(_pieces(c["tn"], NP), expand(c["t"]))
    for c in chains:
        c["bd_v"] = expand(c["v"])
        c["rhs"] = c["p1"][:L] + _mmp(_pieces(c["a_ak"], NP), c["bd_v"])
    for c in chains:
        c["u"] = _mmp(_pieces(c["t"], NP), expand(c["rhs"]))
    for c in chains:
        c["y"] = (c["p1"][L:] + _mmp(_pieces(c["a_rb"], NP), expand(c["u"]))
                  + _mmp(_pieces(c["a_rk"], NP), c["bd_v"]))
        q = _mmp(_pieces(jnp.concatenate([c["u"], c["v"]], axis=0), NP), _pieces(c["bk_f"], NP), _TN) * bd
        c["s_new"] = c["s_old"] * c["g_tot"] + (q[0:N_A] + q[N_A:2 * N_A] + q[2 * N_A:3 * N_A]
                                                + q[3 * N_A:4 * N_A])
    for c in chains:
        bi, sl = c["bi"], c["sl"]
        state_ref[bi, :, sl] = c["s_new"]
        sfin_ref[bi, 0, :, sl] = c["s_new"]
        y_ref[0, bi, :, sl] = c["y"]
        z_ref[0, bi, :, sl] = c["z"]


def _rwkv_scan(pa, s0, w0, w_up, a0, a_up, k_k, k_a, r_k):
    b, t, _ = pa.shape
    nc = t // CHUNK
    strict, incl, cmat, levels, eye, bd = _rwkv_masks()
    zpad = jnp.zeros((2, LORA_W, D_A), jnp.float32)
    wup = jnp.concatenate([w_up, zpad], axis=1)
    aup = jnp.concatenate([zpad, a_up], axis=1)
    chunk = lambda d, j: j + d * (nc - 1 - 2 * j)
    dsel3 = lambda bi, d, j: (d, 0, 0)
    const2 = lambda bi, d, j: (0, 0)
    nb = RWKV_SEQS
    return pl.pallas_call(
        _rwkv_kernel,
        grid=(b // nb, 2, nc),
        in_specs=[pl.BlockSpec((nb, CHUNK, D_A_COLS), lambda bi, d, j: (bi, chunk(d, j), 0)),
                  pl.BlockSpec((nb, 1, N_A, D_A), lambda bi, d, j: (bi, d, 0, 0)),
                  pl.BlockSpec((1, 1, D_A), dsel3),
                  pl.BlockSpec((1, LORA_W + LORA_A, D_A), dsel3),
                  pl.BlockSpec((1, 1, D_A), dsel3),
                  pl.BlockSpec((1, LORA_W + LORA_A, D_A), dsel3),
                  pl.BlockSpec((1, D_A), const2),
                  pl.BlockSpec((1, D_A), const2),
                  pl.BlockSpec((1, D_A), const2),
                  pl.BlockSpec((1, CHUNK, GROUP_LANES), dsel3),
                  pl.BlockSpec((1, CHUNK, GROUP_LANES), dsel3),
                  pl.BlockSpec((1, CHUNK, CHUNK), dsel3),
                  pl.BlockSpec((1, 6, CHUNK, GROUP_LANES), lambda bi, d, j: (d, 0, 0, 0)),
                  pl.BlockSpec((CHUNK, GROUP_LANES), const2),
                  pl.BlockSpec((GROUP_LANES, GROUP_LANES), const2)],
        out_specs=[pl.BlockSpec((1, nb, CHUNK, D_A), lambda bi, d, j: (d, bi, chunk(d, j), 0)),
                   pl.BlockSpec((1, nb, CHUNK, D_A), lambda bi, d, j: (d, bi, chunk(d, j), 0)),
                   pl.BlockSpec((nb, 1, N_A, D_A), lambda bi, d, j: (bi, d, 0, 0))],
        out_shape=[jax.ShapeDtypeStruct((2, b, t, D_A), jnp.float32),
                   jax.ShapeDtypeStruct((2, b, t, D_A), jnp.float32),
                   jax.ShapeDtypeStruct((b, 2, N_A, D_A), jnp.float32)],
        scratch_shapes=[pltpu.VMEM((nb, N_A, D_A), jnp.float32)],
        compiler_params=pltpu.CompilerParams(dimension_semantics=("arbitrary", "arbitrary", "arbitrary"),
                                             vmem_limit_bytes=VMEM_LIMIT),
        name="rwkv_scan",
    )(pa, s0, w0.reshape(2, 1, D_A), wup, a0.reshape(2, 1, D_A), aup,
      k_k.reshape(1, D_A), k_a.reshape(1, D_A), r_k.reshape(1, D_A),
      strict, incl, cmat, levels, eye, bd)


def _softmax_pv(scores, values):
    m = scores[0].max(axis=-1, keepdims=True)
    for s in scores[1:]:
        m = jnp.maximum(m, s.max(axis=-1, keepdims=True))
    den = None
    acc = None
    for s, val in zip(scores, values):
        p = jnp.exp(s - m)
        d = p.sum(axis=-1, keepdims=True)
        o = _dot(p.astype(_BF), val)
        den = d if den is None else den + d
        acc = o if acc is None else acc + o
    return acc / den


def _ctx_attn_kernel(qkv_ref, o_ref):
    for h in range(H_B):
        hs = slice(h * N_B, (h + 1) * N_B)
        q = (qkv_ref[0, :, hs] * ATTN_SCALE).astype(_BF)
        k = qkv_ref[0, :, D_B + h * N_B:D_B + (h + 1) * N_B].astype(_BF)
        v = qkv_ref[0, :, 2 * D_B + h * N_B:2 * D_B + (h + 1) * N_B].astype(_BF)
        o_ref[0, :, hs] = _softmax_pv([_dot(q, k, _NT)], [v])


def _context_attention(qkv):
    b, t, _ = qkv.shape
    return pl.pallas_call(
        _ctx_attn_kernel,
        grid=(b,),
        in_specs=[pl.BlockSpec((1, t, 3 * D_B), lambda i: (i, 0, 0))],
        out_specs=pl.BlockSpec((1, t, D_B), lambda i: (i, 0, 0)),
        out_shape=jax.ShapeDtypeStruct((b, t, D_B), jnp.float32),
        compiler_params=pltpu.CompilerParams(dimension_semantics=("arbitrary",),
                                             vmem_limit_bytes=VMEM_LIMIT),
        name="context_attention",
    )(qkv)


NA_BAND = WIN_R * GRID_W
NA_EDGE = WIN_R // 2


def _na_bias(rpb, rows):
    cfg_rows = list(range(NA_EDGE)) + [NA_EDGE] + list(range(rows - NA_EDGE, rows))
    r_ids = np.asarray(cfg_rows)
    row_start = np.clip(r_ids - WIN_R // 2, 0, rows - WIN_R)
    dr = row_start[:, None] + np.arange(WIN_R)[None, :] - r_ids[:, None] + WIN_R - 1
    c_ids = np.arange(GRID_W)
    col_start = np.clip(c_ids - WIN_C // 2, 0, GRID_W - WIN_C)
    valid = (c_ids[None, :] >= col_start[:, None]) & (c_ids[None, :] < col_start[:, None] + WIN_C)
    dc = np.clip(c_ids[None, :] - c_ids[:, None] + WIN_C - 1, 0, 2 * WIN_C - 2)
    rows_sel = jnp.stack([jnp.stack([rpb[:, int(d)] for d in dr_c], axis=1) for dr_c in dr], axis=1)
    onehot = jnp.asarray((dc.reshape(-1)[None, :] == np.arange(2 * WIN_C - 1)[:, None]).astype(np.float32))
    bias = jnp.dot(rows_sel, onehot, precision=lax.Precision.HIGHEST)
    bias = bias.reshape(H_B, len(cfg_rows), WIN_R, GRID_W, GRID_W)
    bias = jnp.where(valid[None, None, None, :, :], bias, NEG_INF)
    return jnp.transpose(bias, (1, 0, 3, 2, 4)).reshape(len(cfg_rows), H_B, GRID_W, NA_BAND)


def _na_kernel(q_ref, k_ref, v_ref, ck_ref, cv_ref, bias_ref, o_ref):
    r = pl.program_id(1)
    rows = k_ref.shape[1] // GRID_W
    start = pl.multiple_of(jnp.clip(r - WIN_R // 2, 0, rows - WIN_R) * GRID_W, GRID_W)
    for h in range(H_B):
        hs = slice(h * N_B, (h + 1) * N_B)
        q = q_ref[0, :, hs] * ATTN_SCALE
        k = k_ref[0, pl.ds(start, NA_BAND), hs]
        v = v_ref[0, pl.ds(start, NA_BAND), hs]
        s_loc = _dot(q, k, _NT) + bias_ref[0, h]
        s_ctx = _dot(q, ck_ref[0, :, hs], _NT)
        o_ref[0, :, hs] = _softmax_pv([s_loc, s_ctx], [v, cv_ref[0, :, hs]])


def _neighbourhood_attention(qkv_bf16, ctx_k, ctx_v, rpb):
    b, t, _ = qkv_bf16.shape
    rows = t // GRID_W
    past = ctx_k.shape[1]
    bias = _na_bias(rpb, rows)
    cfg = lambda bi, r: (jnp.minimum(r, NA_EDGE) + jnp.maximum(r - (rows - NA_EDGE - 1), 0), 0, 0, 0)
    return pl.pallas_call(
        _na_kernel,
        grid=(b, rows),
        in_specs=[pl.BlockSpec((1, GRID_W, D_B), lambda bi, r: (bi, r, 0)),
                  pl.BlockSpec((1, t, D_B), lambda bi, r: (bi, 0, 1)),
                  pl.BlockSpec((1, t, D_B), lambda bi, r: (bi, 0, 2)),
                  pl.BlockSpec((1, past, D_B), lambda bi, r: (bi, 0, 0)),
                  pl.BlockSpec((1, past, D_B), lambda bi, r: (bi, 0, 0)),
                  pl.BlockSpec((1, H_B, GRID_W, NA_BAND), cfg)],
        out_specs=pl.BlockSpec((1, GRID_W, D_B), lambda bi, r: (bi, r, 0)),
        out_shape=jax.ShapeDtypeStruct((b, t, D_B), jnp.float32),
        compiler_params=pltpu.CompilerParams(dimension_semantics=("arbitrary", "arbitrary"),
                                             vmem_limit_bytes=VMEM_LIMIT),
        name="neighbourhood_attention",
    )(qkv_bf16, qkv_bf16, qkv_bf16, ctx_k.reshape(b, past, D_B).astype(_BF),
      ctx_v.reshape(b, past, D_B).astype(_BF), bias)


def _outproj_kernel(y_ref, z_ref, glo_ref, yb_ref, x_ref, mod_ref, g2_ref, w_ref, router_ref,
                    gup_ref, gng_ref, gnb_ref, bd_ref, x1_ref, h2_ref, logit_ref):
    mod = mod_ref[0]
    bd = bd_ref[...]
    yf = y_ref[0] + y_ref[1]
    mu = _mmp(_pieces(yf, 3), [bd]) * (1.0 / N_A)
    dev = yf - mu
    var = _mmp(_pieces(dev * dev, 3), [bd]) * (1.0 / N_A)
    yn = dev * lax.rsqrt(var + GN_EPS)
    gate = _mmp(_pieces(jax.nn.sigmoid(glo_ref[...]), 2), _pieces(gup_ref[...], 2))
    ya = (yn * gng_ref[...] + gnb_ref[...] + z_ref[0] + z_ref[1]) * gate
    acc = _dot(ya.astype(_BF), w_ref[0:D_A, :]) + _dot(yb_ref[...].astype(_BF), w_ref[D_A:, :])
    x1 = x_ref[...] + mod[2:3] * acc
    x1_ref[...] = x1
    h2 = _rms_mod(x1, g2_ref[...], mod[4:5], mod[3:4])
    h2_ref[...] = h2.astype(_BF)
    logit_ref[...] = _mmp(_pieces(h2, 2), _pieces(router_ref[...], 2))


def _outproj(y, z, pa2d, yb, x2d, mod, mod_base, rows_per_mod, norm2_g, w_out_bf16, router_pad,
             g_up, gn_g, gn_b):
    n = x2d.shape[0]
    tm = ROW_TILE
    mod_idx = lambda i: (mod_base + (i * tm) // rows_per_mod, 0, 0)
    row = lambda i: (i, 0)
    row3 = lambda i: (0, i, 0)
    const = lambda i: (0, 0)
    bd = jnp.asarray(_head_block_mask(D_A).astype(np.float32)).astype(_BF)
    return pl.pallas_call(
        _outproj_kernel,
        grid=(n // tm,),
        in_specs=[pl.BlockSpec((2, tm, D_A), row3), pl.BlockSpec((2, tm, D_A), row3),
                  pl.BlockSpec((tm, LORA_G), lambda i: (i, (D_A_COLS - LORA_G) // LORA_G)),
                  pl.BlockSpec((tm, D_B), row),
                  pl.BlockSpec((tm, D_MODEL), row),
                  pl.BlockSpec((1, 6, D_MODEL), mod_idx),
                  pl.BlockSpec((1, D_MODEL), const),
                  pl.BlockSpec((D_MODEL, D_MODEL), const),
                  pl.BlockSpec((D_MODEL, 128), const),
                  pl.BlockSpec((LORA_G, D_A), const),
                  pl.BlockSpec((1, D_A), const),
                  pl.BlockSpec((1, D_A), const),
                  pl.BlockSpec((D_A, D_A), const)],
        out_specs=[pl.BlockSpec((tm, D_MODEL), row), pl.BlockSpec((tm, D_MODEL), row),
                   pl.BlockSpec((tm, 128), row)],
        out_shape=[jax.ShapeDtypeStruct((n, D_MODEL), jnp.float32),
                   jax.ShapeDtypeStruct((n, D_MODEL), _BF),
                   jax.ShapeDtypeStruct((n, 128), jnp.float32)],
        compiler_params=pltpu.CompilerParams(dimension_semantics=("arbitrary",),
                                             vmem_limit_bytes=VMEM_LIMIT),
        name="outproj",
    )(y, z, pa2d, yb, x2d, mod, norm2_g.reshape(1, D_MODEL), w_out_bf16, router_pad,
      g_up, gn_g.reshape(1, D_A), gn_b.reshape(1, D_A), bd)


MOE_TF = 256
MOE_TM = 512


def _moe_kernel(xp_ref, xs_ref, wg_ref, wu_ref, wd_ref, op_ref, os_ref, wg_bf, wu_bf, wd_bf):
    f = pl.program_id(1)
    wg_bf[...] = wg_ref[0].astype(_BF)
    wu_bf[...] = wu_ref[0].astype(_BF)
    wd_bf[...] = wd_ref[0].astype(_BF)

    for x_ref, o_ref in ((xp_ref, op_ref), (xs_ref, os_ref)):
        def rows(i, carry, x_ref=x_ref, o_ref=o_ref):
            sl = pl.ds(pl.multiple_of(i * MOE_TM, MOE_TM), MOE_TM)
            xe = x_ref[0, sl, :]
            gt = _dot(xe, wg_bf[...])
            up = _dot(xe, wu_bf[...])
            hid = (gt * jax.nn.sigmoid(gt) * up).astype(_BF)
            part = _dot(hid, wd_bf[...])

            @pl.when(f == 0)
            def _():
                o_ref[0, sl, :] = part

            @pl.when(f > 0)
            def _():
                o_ref[0, sl, :] += part

            return carry

        lax.fori_loop(0, x_ref.shape[1] // MOE_TM, rows, 0)


def _moe_experts(xe_p, xe_s, e_gate, e_up, e_down):
    e = xe_p.shape[0]
    nf = D_EXPERT // MOE_TF
    slab = lambda x: pl.BlockSpec((1, x.shape[1], D_MODEL), lambda ei, f: (ei, 0, 0))
    return pl.pallas_call(
        _moe_kernel,
        grid=(e, nf),
        in_specs=[slab(xe_p), slab(xe_s),
                  pl.BlockSpec((1, D_MODEL, MOE_TF), lambda ei, f: (ei, 0, f)),
                  pl.BlockSpec((1, D_MODEL, MOE_TF), lambda ei, f: (ei, 0, f)),
                  pl.BlockSpec((1, MOE_TF, D_MODEL), lambda ei, f: (ei, f, 0))],
        out_specs=[slab(xe_p), slab(xe_s)],
        out_shape=[jax.ShapeDtypeStruct(xe_p.shape, jnp.float32),
                   jax.ShapeDtypeStruct(xe_s.shape, jnp.float32)],
        scratch_shapes=[pltpu.VMEM((D_MODEL, MOE_TF), _BF),
                        pltpu.VMEM((D_MODEL, MOE_TF), _BF),
                        pltpu.VMEM((MOE_TF, D_MODEL), _BF)],
        compiler_params=pltpu.CompilerParams(dimension_semantics=("arbitrary", "arbitrary"),
                                             vmem_limit_bytes=VMEM_LIMIT),
        name="moe_experts",
    )(xe_p, xe_s, e_gate, e_up, e_down)


MOE_SLOTS = 512


def _gather_kernel(idx_ref, h_ref, o_ref):
    t = h_ref.shape[1]
    eb, cap, _ = o_ref.shape
    sel = (lax.broadcasted_iota(jnp.int32, (MOE_SLOTS, t), 1) == idx_ref[0]).astype(_BF)
    rows = _dot(sel, h_ref[0]).astype(_BF)
    for e in range(eb):
        o_ref[e] = rows[e * cap:(e + 1) * cap]


def _moe_gather(h, idx):
    b, t, _ = h.shape
    _, e, cap = idx.shape
    eb = MOE_SLOTS // cap
    return pl.pallas_call(
        _gather_kernel,
        grid=(b, e // eb),
        in_specs=[pl.BlockSpec((1, MOE_SLOTS, 1), lambda bi, eg: (bi, eg, 0)),
                  pl.BlockSpec((1, t, D_MODEL), lambda bi, eg: (bi, 0, 0))],
        out_specs=pl.BlockSpec((eb, cap, D_MODEL), lambda bi, eg: (eg, bi, 0)),
        out_shape=jax.ShapeDtypeStruct((e, b * cap, D_MODEL), _BF),
        compiler_params=pltpu.CompilerParams(dimension_semantics=("arbitrary", "arbitrary"),
                                             vmem_limit_bytes=VMEM_LIMIT),
        name="moe_gather",
    )(idx.reshape(b, e * cap, 1), h)


def _scatter_kernel(idx_ref, gate_ref, ye_ref, o_ref):
    eg = pl.program_id(1)
    t = o_ref.shape[1]
    eb = ye_ref.shape[0]
    ye = jnp.concatenate([ye_ref[e] for e in range(eb)], axis=0) * gate_ref[0]
    ye = ye.astype(_BF)
    tt = min(t, ROW_TILE)
    for t0 in range(0, t, tt):
        sel = (lax.broadcasted_iota(jnp.int32, (tt, MOE_SLOTS), 0) + t0 == idx_ref[0]).astype(_BF)
        part = _dot(sel, ye)

        @pl.when(eg == 0)
        def _():
            o_ref[0, t0:t0 + tt, :] = part

        @pl.when(eg > 0)
        def _():
            o_ref[0, t0:t0 + tt, :] += part


def _moe_scatter(ye, idx, gate, t):
    b, e, cap = idx.shape
    eb = MOE_SLOTS // cap
    return pl.pallas_call(
        _scatter_kernel,
        grid=(b, e // eb),
        in_specs=[pl.BlockSpec((1, 1, MOE_SLOTS), lambda bi, eg: (bi, 0, eg)),
                  pl.BlockSpec((1, MOE_SLOTS, 1), lambda bi, eg: (bi, eg, 0)),
                  pl.BlockSpec((eb, cap, D_MODEL), lambda bi, eg: (eg, bi, 0))],
        out_specs=pl.BlockSpec((1, t, D_MODEL), lambda bi, eg: (bi, 0, 0)),
        out_shape=jax.ShapeDtypeStruct((b, t, D_MODEL), jnp.float32),
        compiler_params=pltpu.CompilerParams(dimension_semantics=("arbitrary", "arbitrary"),
                                             vmem_limit_bytes=VMEM_LIMIT),
        name="moe_scatter",
    )(idx.reshape(b, 1, e * cap), gate.reshape(b, e * cap, 1), ye)


def _final_kernel(x1_ref, moe_ref, mod_ref, g_ref, o_ref):
    x2 = x1_ref[...] + mod_ref[0][5:6] * moe_ref[...]
    o_ref[...] = x2 * lax.rsqrt(jnp.mean(x2 * x2, axis=-1, keepdims=True) + RMS_EPS) * g_ref[...]


def _final(x1, moe, mod, mod_base, rows_per_mod, final_g):
    n = x1.shape[0]
    tm = ROW_TILE
    mod_idx = lambda i: (mod_base + (i * tm) // rows_per_mod, 0, 0)
    row = lambda i: (i, 0)
    return pl.pallas_call(
        _final_kernel,
        grid=(n // tm,),
        in_specs=[pl.BlockSpec((tm, D_MODEL), row), pl.BlockSpec((tm, D_MODEL), row),
                  pl.BlockSpec((1, 6, D_MODEL), mod_idx),
                  pl.BlockSpec((1, D_MODEL), lambda i: (0, 0))],
        out_specs=pl.BlockSpec((tm, D_MODEL), row),
        out_shape=jax.ShapeDtypeStruct((n, D_MODEL), jnp.float32),
        compiler_params=pltpu.CompilerParams(dimension_semantics=("arbitrary",),
                                             vmem_limit_bytes=VMEM_LIMIT),
        name="final_norm",
    )(x1, moe, mod, final_g.reshape(1, D_MODEL))


def _centred_shift(p, mu_prev, mu_next):
    zero = jnp.zeros_like(p[:, :1])
    p_prev = jnp.concatenate([zero, p[:, :-1]], axis=1)
    p_next = jnp.concatenate([p[:, 1:], zero], axis=1)
    return p + mu_prev * (p_prev - p) + mu_next * (p_next - p)


def _route(logits, b, t):
    cap = EC_FACTOR * t // N_EXPERTS
    aff = jax.nn.softmax(logits.reshape(b, t, N_EXPERTS), axis=-1)
    gate, idx = lax.top_k(jnp.swapaxes(aff, 1, 2), cap)
    return gate, idx


def kernel(x_prompt, x_sample, cache_na_k, cache_na_v, state_rwkv, c, c_ctx, final_norm_g, norm1_g, norm2_g,
           w_mod, b_mod, w_in, mu_prev, mu_next, w0, w_up, a0, a_up, g_up, k_k, k_a, r_k, gn_g, gn_b, rpb,
           w_out, router, e_gate, e_up, e_down):
    bp, tp, _ = x_prompt.shape
    bs, ts, _ = x_sample.shape
    l = 0
    cond8 = jnp.concatenate([c_ctx[None, :], c, jnp.zeros((8 - 1 - bs, D_MODEL), jnp.float32)], axis=0)
    mod = _adaln(cond8, w_mod[l], b_mod[l]).reshape(8, 6, D_MODEL)
    w_in_bf = w_in[l].astype(_BF)
    w_out_bf = w_out[l].astype(_BF)
    router_pad = jnp.pad(router[l], ((0, 0), (0, 128 - N_EXPERTS)))

    groups = []
    for name, x, mod_base, s0 in (
            ("prompt", x_prompt, 0, jnp.zeros((bp, 2, N_A, D_A), jnp.float32)),
            ("sample", x_sample, 1,
             jnp.transpose(state_rwkv[:, l], (0, 1, 3, 2, 4)).reshape(bs, 2, N_A, D_A))):
        b, t, _ = x.shape
        is_prompt = name == "prompt"
        rows_per_mod = b * t if is_prompt else t
        x2d = x.reshape(b * t, D_MODEL)
        pa, qkv = _inproj(x2d, mod, mod_base, rows_per_mod, norm1_g[l], w_in_bf,
                          jnp.float32 if is_prompt else _BF)
        pa = _centred_shift(pa.reshape(b, t, D_A_COLS), mu_prev[l], mu_next[l])
        qkv = qkv.reshape(b, t, 3 * D_B)
        y, z, s_fin = _rwkv_scan(pa, s0, w0[l], w_up[l], a0[l], a_up[l], k_k[l], k_a[l], r_k[l].reshape(D_A))
        if is_prompt:
            yb = _context_attention(qkv)
        else:
            yb = _neighbourhood_attention(qkv, cache_na_k[:, l], cache_na_v[:, l], rpb[l])
        x1, h2, logits = _outproj(y.reshape(2, b * t, D_A), z.reshape(2, b * t, D_A),
                                  pa.reshape(b * t, D_A_COLS), yb.reshape(b * t, D_B), x2d, mod, mod_base,
                                  rows_per_mod, norm2_g[l], w_out_bf, router_pad, g_up[l], gn_g[l], gn_b[l])
        gate, idx = _route(logits[:, :N_EXPERTS], b, t)
        groups.append(dict(b=b, t=t, x1=x1, h2=h2, gate=gate, idx=idx, qkv=qkv, s_fin=s_fin,
                           mod_base=mod_base, rows_per_mod=rows_per_mod))

    xes = [_moe_gather(gr["h2"].reshape(gr["b"], gr["t"], D_MODEL), gr["idx"]) for gr in groups]
    yes = _moe_experts(xes[0], xes[1], e_gate[l], e_up[l], e_down[l])

    outs = []
    for gr, ye in zip(groups, yes):
        b, t = gr["b"], gr["t"]
        moe = _moe_scatter(ye, gr["idx"], gr["gate"], t)
        y = _final(gr["x1"], moe.reshape(b * t, D_MODEL), mod, gr["mod_base"], gr["rows_per_mod"], final_norm_g)
        outs.append(y.reshape(b, t, D_MODEL))

    gp = groups[0]
    new_k = gp["qkv"][:, :, D_B:2 * D_B].reshape(bp, 1, tp, H_B, N_B)
    new_v = gp["qkv"][:, :, 2 * D_B:].reshape(bp, 1, tp, H_B, N_B)
    new_s = jnp.transpose(gp["s_fin"].reshape(bp, 2, N_A, H_A, N_A), (0, 1, 3, 2, 4))
    return (outs[0], outs[1], new_k, new_v, new_s[:, None])
```

```python
import numpy as np
import jax
import jax.numpy as jnp
from jax import lax
from jax.experimental import pallas as pl
from jax.experimental.pallas import tpu as pltpu

D_MODEL = 1024
GRID_W = 64
H_A = 8
N_A = 64
D_A = H_A * N_A
H_B = 8
N_B = 64
D_B = H_B * N_B
LORA_W = 64
LORA_A = 64
LORA_G = 128
D_A_COLS = 3 * D_A + LORA_W + LORA_A + LORA_G
D_IN = D_A_COLS + 3 * D_B
WIN_R = 8
WIN_C = 16
N_EXPERTS = 16
EC_FACTOR = 2
D_EXPERT = 2816
RMS_EPS = 1e-6
GN_EPS = 64e-5
ATTN_SCALE = N_B ** -0.5
NEG_INF = -1e30

CHUNK = 64
HEAD_GROUP = 4
GROUP_LANES = HEAD_GROUP * N_A
ROW_TILE = 512
VMEM_LIMIT = 56 * 1024 * 1024

_NN = (((1,), (0,)), ((), ()))
_NT = (((1,), (1,)), ((), ()))
_TN = (((0,), (0,)), ((), ()))
_BF = jnp.bfloat16


def _dot(a, b, dims=_NN):
    return lax.dot_general(a, b, dims, preferred_element_type=jnp.float32)


def _pieces(a, n):
    out = []
    for i in range(n):
        p = a.astype(_BF)
        out.append(p)
        if i + 1 < n:
            a = a - p.astype(jnp.float32)
    return out


def _mmp(ap, bp, dims=_NN):
    n = max(len(ap), len(bp))
    acc = None
    for i, x in enumerate(ap):
        for j, y in enumerate(bp):
            if i + j < n:
                d = _dot(x, y, dims)
                acc = d if acc is None else acc + d
    return acc


def _mod_kernel(c_ref, w_ref, b_ref, o_ref):
    c = c_ref[...]
    s = c * jax.nn.sigmoid(c)
    o_ref[...] = _dot(s.astype(_BF), w_ref[...].astype(_BF)) + b_ref[...]


def _adaln(cond8, w_mod, b_mod):
    n = w_mod.shape[1]
    tn = 1024
    return pl.pallas_call(
        _mod_kernel,
        grid=(n // tn,),
        in_specs=[pl.BlockSpec((8, D_MODEL), lambda j: (0, 0)),
                  pl.BlockSpec((D_MODEL, tn), lambda j: (0, j)),
                  pl.BlockSpec((1, tn), lambda j: (0, j))],
        out_specs=pl.BlockSpec((8, tn), lambda j: (0, j)),
        out_shape=jax.ShapeDtypeStruct((8, n), jnp.float32),
        compiler_params=pltpu.CompilerParams(dimension_semantics=("arbitrary",),
                                             vmem_limit_bytes=VMEM_LIMIT),
        name="adaln_mod",
    )(cond8, w_mod, b_mod.reshape(1, n))


def _rms_mod(x, g, scale, shift):
    y = x * lax.rsqrt(jnp.mean(x * x, axis=-1, keepdims=True) + RMS_EPS)
    return (y * g) * (1.0 + scale) + shift


def _inproj_kernel(x_ref, mod_ref, g_ref, w_ref, pa_ref, qkv_ref):
    mod = mod_ref[0]
    h = _rms_mod(x_ref[...], g_ref[...], mod[1:2], mod[0:1]).astype(_BF)
    for n0 in range(0, D_A_COLS, 256):
        pa_ref[:, n0:n0 + 256] = _dot(h, w_ref[:, n0:n0 + 256])
    for n0 in range(0, 3 * D_B, 256):
        qkv_ref[:, n0:n0 + 256] = _dot(h, w_ref[:, D_A_COLS + n0:D_A_COLS + n0 + 256]).astype(qkv_ref.dtype)


def _inproj(x2d, mod, mod_base, rows_per_mod, norm_g, w_in_bf16, qkv_dtype):
    n = x2d.shape[0]
    tm = ROW_TILE
    mod_idx = lambda i: (mod_base + (i * tm) // rows_per_mod, 0, 0)
    return pl.pallas_call(
        _inproj_kernel,
        grid=(n // tm,),
        in_specs=[pl.BlockSpec((tm, D_MODEL), lambda i: (i, 0)),
                  pl.BlockSpec((1, 6, D_MODEL), mod_idx),
                  pl.BlockSpec((1, D_MODEL), lambda i: (0, 0)),
                  pl.BlockSpec((D_MODEL, D_IN), lambda i: (0, 0))],
        out_specs=[pl.BlockSpec((tm, D_A_COLS), lambda i: (i, 0)),
                   pl.BlockSpec((tm, 3 * D_B), lambda i: (i, 0))],
        out_shape=[jax.ShapeDtypeStruct((n, D_A_COLS), jnp.float32),
                   jax.ShapeDtypeStruct((n, 3 * D_B), qkv_dtype)],
        compiler_params=pltpu.CompilerParams(dimension_semantics=("arbitrary",),
                                             vmem_limit_bytes=VMEM_LIMIT),
        name="inproj",
    )(x2d, mod, norm_g.reshape(1, D_MODEL), w_in_bf16)


RWKV_PIECES = 1
RWKV_SEQS = 4


def _head_block_mask(n):
    r = np.arange(n)
    return (r[:, None] // N_A) == (r[None, :] // N_A)


def _rwkv_masks():
    L = CHUNK
    i = np.arange(L)[:, None]
    j = np.arange(L)[None, :]
    strict, incl, cmat, levels = [], [], [], []
    for rev in (False, True):
        before = (j > i) if rev else (j < i)
        strict.append(np.tile(before, (1, HEAD_GROUP)))
        incl.append(np.tile(before | (i == j), (1, HEAD_GROUP)))
        cmat.append(before | (i == j))
        lv = []
        s = 1
        while s < L:
            same = (i // (2 * s)) == (j // (2 * s))
            late_i, early_j = (i % (2 * s)) >= s, (j % (2 * s)) < s
            m = same & late_i & early_j
            if rev:
                m = m.T
            lv.append(np.tile(m, (1, HEAD_GROUP)))
            s *= 2
        levels.append(np.stack(lv))
    f = lambda a: jnp.asarray(np.stack(a).astype(np.float32))
    eye = jnp.asarray(np.tile(np.eye(L, dtype=np.float32), (1, HEAD_GROUP)))
    bd = jnp.asarray(_head_block_mask(GROUP_LANES).astype(np.float32)).astype(_BF)
    return f(strict), f(incl), f(cmat).astype(_BF), f(levels), eye, bd


def _softplus(z):
    return jnp.maximum(z, 0.0) + jnp.log1p(jnp.exp(-jnp.abs(z)))


def _rwkv_kernel(pa_ref, s0_ref, w0_ref, wup_ref, a0_ref, aup_ref, kk_ref, ka_ref, rk_ref,
                 strict_ref, incl_ref, cmat_ref, lvl_ref, eye_ref, bd_ref,
                 y_ref, z_ref, sfin_ref, state_ref):
    j = pl.program_id(2)

    @pl.when(j == 0)
    def _():
        state_ref[...] = s0_ref[:, 0]

    L = CHUNK
    NP = RWKV_PIECES
    strict = strict_ref[0]
    incl = incl_ref[0]
    bd = bd_ref[...]
    eye = eye_ref[...]

    def expand(m, n=NP):
        return [jnp.concatenate([p] * HEAD_GROUP, axis=0) * bd for p in _pieces(m, n)]

    chains = []
    for bi in range(pa_ref.shape[0]):
        r = pa_ref[bi, :, 0:D_A]
        k = pa_ref[bi, :, D_A:2 * D_A]
        v = pa_ref[bi, :, 2 * D_A:3 * D_A]
        lo = pa_ref[bi, :, 3 * D_A:3 * D_A + LORA_W + LORA_A]
        wz = w0_ref[0] + _mmp(_pieces(jnp.tanh(lo), 2), _pieces(wup_ref[0], 2))
        lw = -jnp.exp(-_softplus(-wz) - 0.5)
        a = jax.nn.sigmoid(a0_ref[0] + _mmp(_pieces(lo, 2), _pieces(aup_ref[0], 2)))
        kd = k * (1.0 + (a - 1.0) * ka_ref[...])
        kkf = k * kk_ref[...]
        cum = _mmp([cmat_ref[0]], _pieces(lw, 3))
        tot = jnp.sum(lw, axis=0, keepdims=True)
        e_in = jnp.exp(cum)
        e_ex = jnp.exp(cum - lw)
        e_inv = jnp.exp(-cum)
        e_fin = jnp.exp(tot - cum)
        g_tot = jnp.exp(tot)
        for g in range(H_A // HEAD_GROUP):
            sl = slice(g * GROUP_LANES, (g + 1) * GROUP_LANES)
            kkf_g = kkf[:, sl]
            ss = _mmp(_pieces(kkf_g * kkf_g, 2), [bd])
            kk = kkf_g * lax.rsqrt(ss + 1e-12)
            bvec = kk * a[:, sl]
            r_g, kd_g, v_g = r[:, sl], kd[:, sl], v[:, sl]
            chains.append(dict(
                bi=bi, sl=sl, v=v_g, g_tot=g_tot[:, sl],
                x_ar=_pieces(jnp.concatenate([-kk * e_ex[:, sl], r_g * e_in[:, sl]], axis=0), NP),
                b_t=bvec * e_inv[:, sl], k_t=kd_g * e_inv[:, sl],
                bk_f=jnp.concatenate([bvec * e_fin[:, sl], kd_g * e_fin[:, sl]], axis=0),
                z=_mmp(_pieces(r_g * kd_g * rk_ref[:, sl], 2), [bd]) * v_g,
                s_old=state_ref[bi, :, sl]))

    for c in chains:
        c["p1"] = _mmp(c["x_ar"], expand(c["s_old"]), _NT)
        xb = _mmp(c["x_ar"], expand(c["b_t"]), _NT)
        xk = _mmp(c["x_ar"], expand(c["k_t"]), _NT)
        c["n_ab"] = xb[:L] * strict
        c["a_ak"] = xk[:L] * strict
        c["a_rb"] = xb[L:] * incl
        c["a_rk"] = xk[L:] * incl
        c["t"] = eye + c["n_ab"] * lvl_ref[0, 0]
    for lv in range(1, 6):
        for c in chains:
            c["tn"] = _mmp(_pieces(c["t"], NP), expand(c["n_ab"] * lvl_ref[0, lv]))
        for c in chains:
            c["t"] = c["t"] + _mmp(_pieces(c["tn"], NP), expand(c["t"]))
    for c in chains:
        c["bd_v"] = expand(c["v"])
        c["rhs"] = c["p1"][:L] + _mmp(_pieces(c["a_ak"], NP), c["bd_v"])
    for c in chains:
        c["u"] = _mmp(_pieces(c["t"], NP), expand(c["rhs"]))
    for c in chains:
        c["y"] = (c["p1"][L:] + _mmp(_pieces(c["a_rb"], NP), expand(c["u"]))
                  + _mmp(_pieces(c["a_rk"], NP), c["bd_v"]))
        q = _mmp(_pieces(jnp.concatenate([c["u"], c["v"]], axis=0), NP), _pieces(c["bk_f"], NP), _TN) * bd
        c["s_new"] = c["s_old"] * c["g_tot"] + (q[0:N_A] + q[N_A:2 * N_A] + q[2 * N_A:3 * N_A]
                                                + q[3 * N_A:4 * N_A])
    for c in chains:
        bi, sl = c["bi"], c["sl"]
        state_ref[bi, :, sl] = c["s_new"]
        sfin_ref[bi, 0, :, sl] = c["s_new"]
        y_ref[0, bi, :, sl] = c["y"]
        z_ref[0, bi, :, sl] = c["z"]


def _rwkv_scan(pa, s0, w0, w_up, a0, a_up, k_k, k_a, r_k):
    b, t, _ = pa.shape
    nc = t // CHUNK
    strict, incl, cmat, levels, eye, bd = _rwkv_masks()
    zpad = jnp.zeros((2, LORA_W, D_A), jnp.float32)
    wup = jnp.concatenate([w_up, zpad], axis=1)
    aup = jnp.concatenate([zpad, a_up], axis=1)
    chunk = lambda d, j: j + d * (nc - 1 - 2 * j)
    dsel3 = lambda bi, d, j: (d, 0, 0)
    const2 = lambda bi, d, j: (0, 0)
    nb = RWKV_SEQS
    return pl.pallas_call(
        _rwkv_kernel,
        grid=(b // nb, 2, nc),
        in_specs=[pl.BlockSpec((nb, CHUNK, D_A_COLS), lambda bi, d, j: (bi, chunk(d, j), 0)),
                  pl.BlockSpec((nb, 1, N_A, D_A), lambda bi, d, j: (bi, d, 0, 0)),
                  pl.BlockSpec((1, 1, D_A), dsel3),
                  pl.BlockSpec((1, LORA_W + LORA_A, D_A), dsel3),
                  pl.BlockSpec((1, 1, D_A), dsel3),
                  pl.BlockSpec((1, LORA_W + LORA_A, D_A), dsel3),
                  pl.BlockSpec((1, D_A), const2),
                  pl.BlockSpec((1, D_A), const2),
                  pl.BlockSpec((1, D_A), const2),
                  pl.BlockSpec((1, CHUNK, GROUP_LANES), dsel3),
                  pl.BlockSpec((1, CHUNK, GROUP_LANES), dsel3),
                  pl.BlockSpec((1, CHUNK, CHUNK), dsel3),
                  pl.BlockSpec((1, 6, CHUNK, GROUP_LANES), lambda bi, d, j: (d, 0, 0, 0)),
                  pl.BlockSpec((CHUNK, GROUP_LANES), const2),
                  pl.BlockSpec((GROUP_LANES, GROUP_LANES), const2)],
        out_specs=[pl.BlockSpec((1, nb, CHUNK, D_A), lambda bi, d, j: (d, bi, chunk(d, j), 0)),
                   pl.BlockSpec((1, nb, CHUNK, D_A), lambda bi, d, j: (d, bi, chunk(d, j), 0)),
                   pl.BlockSpec((nb, 1, N_A, D_A), lambda bi, d, j: (bi, d, 0, 0))],
        out_shape=[jax.ShapeDtypeStruct((2, b, t, D_A), jnp.float32),
                   jax.ShapeDtypeStruct((2, b, t, D_A), jnp.float32),
                   jax.ShapeDtypeStruct((b, 2, N_A, D_A), jnp.float32)],
        scratch_shapes=[pltpu.VMEM((nb, N_A, D_A), jnp.float32)],
        compiler_params=pltpu.CompilerParams(dimension_semantics=("arbitrary", "arbitrary", "arbitrary"),
                                             vmem_limit_bytes=VMEM_LIMIT),
        name="rwkv_scan",
    )(pa, s0, w0.reshape(2, 1, D_A), wup, a0.reshape(2, 1, D_A), aup,
      k_k.reshape(1, D_A), k_a.reshape(1, D_A), r_k.reshape(1, D_A),
      strict, incl, cmat, levels, eye, bd)


def _softmax_pv(scores, values):
    m = scores[0].max(axis=-1, keepdims=True)
    for s in scores[1:]:
        m = jnp.maximum(m, s.max(axis=-1, keepdims=True))
    den = None
    acc = None
    for s, val in zip(scores, values):
        p = jnp.exp(s - m)
        d = p.sum(axis=-1, keepdims=True)
        o = _dot(p.astype(_BF), val)
        den = d if den is None else den + d
        acc = o if acc is None else acc + o
    return acc / den


def _ctx_attn_kernel(qkv_ref, o_ref):
    for h in range(H_B):
        hs = slice(h * N_B, (h + 1) * N_B)
        q = (qkv_ref[0, :, hs] * ATTN_SCALE).astype(_BF)
        k = qkv_ref[0, :, D_B + h * N_B:D_B + (h + 1) * N_B].astype(_BF)
        v = qkv_ref[0, :, 2 * D_B + h * N_B:2 * D_B + (h + 1) * N_B].astype(_BF)
        o_ref[0, :, hs] = _softmax_pv([_dot(q, k, _NT)], [v])


def _context_attention(qkv):
    b, t, _ = qkv.shape
    return pl.pallas_call(
        _ctx_attn_kernel,
        grid=(b,),
        in_specs=[pl.BlockSpec((1, t, 3 * D_B), lambda i: (i, 0, 0))],
        out_specs=pl.BlockSpec((1, t, D_B), lambda i: (i, 0, 0)),
        out_shape=jax.ShapeDtypeStruct((b, t, D_B), jnp.float32),
        compiler_params=pltpu.CompilerParams(dimension_semantics=("arbitrary",),
                                             vmem_limit_bytes=VMEM_LIMIT),
        name="context_attention",
    )(qkv)


NA_BAND = WIN_R * GRID_W
NA_EDGE = WIN_R // 2


def _na_bias(rpb, rows):
    cfg_rows = list(range(NA_EDGE)) + [NA_EDGE] + list(range(rows - NA_EDGE, rows))
    r_ids = np.asarray(cfg_rows)
    row_start = np.clip(r_ids - WIN_R // 2, 0, rows - WIN_R)
    dr = row_start[:, None] + np.arange(WIN_R)[None, :] - r_ids[:, None] + WIN_R - 1
    c_ids = np.arange(GRID_W)
    col_start = np.clip(c_ids - WIN_C // 2, 0, GRID_W - WIN_C)
    valid = (c_ids[None, :] >= col_start[:, None]) & (c_ids[None, :] < col_start[:, None] + WIN_C)
    dc = np.clip(c_ids[None, :] - c_ids[:, None] + WIN_C - 1, 0, 2 * WIN_C - 2)
    rows_sel = jnp.stack([jnp.stack([rpb[:, int(d)] for d in dr_c], axis=1) for dr_c in dr], axis=1)
    onehot = jnp.asarray((dc.reshape(-1)[None, :] == np.arange(2 * WIN_C - 1)[:, None]).astype(np.float32))
    bias = jnp.dot(rows_sel, onehot, precision=lax.Precision.HIGHEST)
    bias = bias.reshape(H_B, len(cfg_rows), WIN_R, GRID_W, GRID_W)
    bias = jnp.where(valid[None, None, None, :, :], bias, NEG_INF)
    return jnp.transpose(bias, (1, 0, 3, 2, 4)).reshape(len(cfg_rows), H_B, GRID_W, NA_BAND)


def _na_kernel(q_ref, k_ref, v_ref, ck_ref, cv_ref, bias_ref, o_ref):
    r = pl.program_id(1)
    rows = k_ref.shape[1] // GRID_W
    start = pl.multiple_of(jnp.clip(r - WIN_R // 2, 0, rows - WIN_R) * GRID_W, GRID_W)
    for h in range(H_B):
        hs = slice(h * N_B, (h + 1) * N_B)
        q = q_ref[0, :, hs] * ATTN_SCALE
        k = k_ref[0, pl.ds(start, NA_BAND), hs]
        v = v_ref[0, pl.ds(start, NA_BAND), hs]
        s_loc = _dot(q, k, _NT) + bias_ref[0, h]
        s_ctx = _dot(q, ck_ref[0, :, hs], _NT)
        o_ref[0, :, hs] = _softmax_pv([s_loc, s_ctx], [v, cv_ref[0, :, hs]])


def _neighbourhood_attention(qkv_bf16, ctx_k, ctx_v, rpb):
    b, t, _ = qkv_bf16.shape
    rows = t // GRID_W
    past = ctx_k.shape[1]
    bias = _na_bias(rpb, rows)
    cfg = lambda bi, r: (jnp.minimum(r, NA_EDGE) + jnp.maximum(r - (rows - NA_EDGE - 1), 0), 0, 0, 0)
    return pl.pallas_call(
        _na_kernel,
        grid=(b, rows),
        in_specs=[pl.BlockSpec((1, GRID_W, D_B), lambda bi, r: (bi, r, 0)),
                  pl.BlockSpec((1, t, D_B), lambda bi, r: (bi, 0, 1)),
                  pl.BlockSpec((1, t, D_B), lambda bi, r: (bi, 0, 2)),
                  pl.BlockSpec((1, past, D_B), lambda bi, r: (bi, 0, 0)),
                  pl.BlockSpec((1, past, D_B), lambda bi, r: (bi, 0, 0)),
                  pl.BlockSpec((1, H_B, GRID_W, NA_BAND), cfg)],
        out_specs=pl.BlockSpec((1, GRID_W, D_B), lambda bi, r: (bi, r, 0)),
        out_shape=jax.ShapeDtypeStruct((b, t, D_B), jnp.float32),
        compiler_params=pltpu.CompilerParams(dimension_semantics=("arbitrary", "arbitrary"),
                                             vmem_limit_bytes=VMEM_LIMIT),
        name="neighbourhood_attention",
    )(qkv_bf16, qkv_bf16, qkv_bf16, ctx_k.reshape(b, past, D_B).astype(_BF),
      ctx_v.reshape(b, past, D_B).astype(_BF), bias)


def _outproj_kernel(y_ref, z_ref, glo_ref, yb_ref, x_ref, mod_ref, g2_ref, w_ref, router_ref,
                    gup_ref, gng_ref, gnb_ref, bd_ref, x1_ref, h2_ref, logit_ref):
    mod = mod_ref[0]
    bd = bd_ref[...]
    yf = y_ref[0] + y_ref[1]
    mu = _mmp(_pieces(yf, 3), [bd]) * (1.0 / N_A)
    dev = yf - mu
    var = _mmp(_pieces(dev * dev, 3), [bd]) * (1.0 / N_A)
    yn = dev * lax.rsqrt(var + GN_EPS)
    gate = _mmp(_pieces(jax.nn.sigmoid(glo_ref[...]), 2), _pieces(gup_ref[...], 2))
    ya = (yn * gng_ref[...] + gnb_ref[...] + z_ref[0] + z_ref[1]) * gate
    acc = _dot(ya.astype(_BF), w_ref[0:D_A, :]) + _dot(yb_ref[...].astype(_BF), w_ref[D_A:, :])
    x1 = x_ref[...] + mod[2:3] * acc
    x1_ref[...] = x1
    h2 = _rms_mod(x1, g2_ref[...], mod[4:5], mod[3:4])
    h2_ref[...] = h2.astype(_BF)
    logit_ref[...] = _mmp(_pieces(h2, 2), _pieces(router_ref[...], 2))


def _outproj(y, z, pa2d, yb, x2d, mod, mod_base, rows_per_mod, norm2_g, w_out_bf16, router_pad,
             g_up, gn_g, gn_b):
    n = x2d.shape[0]
    tm = ROW_TILE
    mod_idx = lambda i: (mod_base + (i * tm) // rows_per_mod, 0, 0)
    row = lambda i: (i, 0)
    row3 = lambda i: (0, i, 0)
    const = lambda i: (0, 0)
    bd = jnp.asarray(_head_block_mask(D_A).astype(np.float32)).astype(_BF)
    return pl.pallas_call(
        _outproj_kernel,
        grid=(n // tm,),
        in_specs=[pl.BlockSpec((2, tm, D_A), row3), pl.BlockSpec((2, tm, D_A), row3),
                  pl.BlockSpec((tm, LORA_G), lambda i: (i, (D_A_COLS - LORA_G) // LORA_G)),
                  pl.BlockSpec((tm, D_B), row),
                  pl.BlockSpec((tm, D_MODEL), row),
                  pl.BlockSpec((1, 6, D_MODEL), mod_idx),
                  pl.BlockSpec((1, D_MODEL), const),
                  pl.BlockSpec((D_MODEL, D_MODEL), const),
                  pl.BlockSpec((D_MODEL, 128), const),
                  pl.BlockSpec((LORA_G, D_A), const),
                  pl.BlockSpec((1, D_A), const),
                  pl.BlockSpec((1, D_A), const),
                  pl.BlockSpec((D_A, D_A), const)],
        out_specs=[pl.BlockSpec((tm, D_MODEL), row), pl.BlockSpec((tm, D_MODEL), row),
                   pl.BlockSpec((tm, 128), row)],
        out_shape=[jax.ShapeDtypeStruct((n, D_MODEL), jnp.float32),
                   jax.ShapeDtypeStruct((n, D_MODEL), _BF),
                   jax.ShapeDtypeStruct((n, 128), jnp.float32)],
        compiler_params=pltpu.CompilerParams(dimension_semantics=("arbitrary",),
                                             vmem_limit_bytes=VMEM_LIMIT),
        name="outproj",
    )(y, z, pa2d, yb, x2d, mod, norm2_g.reshape(1, D_MODEL), w_out_bf16, router_pad,
      g_up, gn_g.reshape(1, D_A), gn_b.reshape(1, D_A), bd)


MOE_TF = 256
MOE_TM = 1024


def _moe_kernel(xp_ref, xs_ref, wg_ref, wu_ref, wd_ref, op_ref, os_ref, wg_bf, wu_bf, wd_bf):
    f = pl.program_id(1)
    wg_bf[...] = wg_ref[0].astype(_BF)
    wu_bf[...] = wu_ref[0].astype(_BF)
    wd_bf[...] = wd_ref[0].astype(_BF)

    for x_ref, o_ref in ((xp_ref, op_ref), (xs_ref, os_ref)):
        def rows(i, carry, x_ref=x_ref, o_ref=o_ref):
            sl = pl.ds(pl.multiple_of(i * MOE_TM, MOE_TM), MOE_TM)
            xe = x_ref[0, sl, :]
            gt = _dot(xe, wg_bf[...])
            up = _dot(xe, wu_bf[...])
            hid = (gt * jax.nn.sigmoid(gt) * up).astype(_BF)
            part = _dot(hid, wd_bf[...])

            @pl.when(f == 0)
            def _():
                o_ref[0, sl, :] = part

            @pl.when(f > 0)
            def _():
                o_ref[0, sl, :] += part

            return carry

        lax.fori_loop(0, x_ref.shape[1] // MOE_TM, rows, 0)


def _moe_experts(xe_p, xe_s, e_gate, e_up, e_down):
    e = xe_p.shape[0]
    nf = D_EXPERT // MOE_TF
    slab = lambda x: pl.BlockSpec((1, x.shape[1], D_MODEL), lambda ei, f: (ei, 0, 0))
    return pl.pallas_call(
        _moe_kernel,
        grid=(e, nf),
        in_specs=[slab(xe_p), slab(xe_s),
                  pl.BlockSpec((1, D_MODEL, MOE_TF), lambda ei, f: (ei, 0, f)),
                  pl.BlockSpec((1, D_MODEL, MOE_TF), lambda ei, f: (ei, 0, f)),
                  pl.BlockSpec((1, MOE_TF, D_MODEL), lambda ei, f: (ei, f, 0))],
        out_specs=[slab(xe_p), slab(xe_s)],
        out_shape=[jax.ShapeDtypeStruct(xe_p.shape, jnp.float32),
                   jax.ShapeDtypeStruct(xe_s.shape, jnp.float32)],
        scratch_shapes=[pltpu.VMEM((D_MODEL, MOE_TF), _BF),
                        pltpu.VMEM((D_MODEL, MOE_TF), _BF),
                        pltpu.VMEM((MOE_TF, D_MODEL), _BF)],
        compiler_params=pltpu.CompilerParams(dimension_semantics=("arbitrary", "arbitrary"),
                                             vmem_limit_bytes=VMEM_LIMIT),
        name="moe_experts",
    )(xe_p, xe_s, e_gate, e_up, e_down)


MOE_SLOTS = 512


def _gather_kernel(idx_ref, h_ref, o_ref):
    t = h_ref.shape[1]
    eb, cap, _ = o_ref.shape
    sel = (lax.broadcasted_iota(jnp.int32, (MOE_SLOTS, t), 1) == idx_ref[0]).astype(_BF)
    rows = _dot(sel, h_ref[0]).astype(_BF)
    for e in range(eb):
        o_ref[e] = rows[e * cap:(e + 1) * cap]


def _moe_gather(h, idx):
    b, t, _ = h.shape
    _, e, cap = idx.shape
    eb = MOE_SLOTS // cap
    return pl.pallas_call(
        _gather_kernel,
        grid=(b, e // eb),
        in_specs=[pl.BlockSpec((1, MOE_SLOTS, 1), lambda bi, eg: (bi, eg, 0)),
                  pl.BlockSpec((1, t, D_MODEL), lambda bi, eg: (bi, 0, 0))],
        out_specs=pl.BlockSpec((eb, cap, D_MODEL), lambda bi, eg: (eg, bi, 0)),
        out_shape=jax.ShapeDtypeStruct((e, b * cap, D_MODEL), _BF),
        compiler_params=pltpu.CompilerParams(dimension_semantics=("arbitrary", "arbitrary"),
                                             vmem_limit_bytes=VMEM_LIMIT),
        name="moe_gather",
    )(idx.reshape(b, e * cap, 1), h)


def _scatter_kernel(idx_ref, gate_ref, ye_ref, o_ref):
    eg = pl.program_id(1)
    t = o_ref.shape[1]
    eb = ye_ref.shape[0]
    ye = jnp.concatenate([ye_ref[e] for e in range(eb)], axis=0) * gate_ref[0]
    ye = ye.astype(_BF)
    tt = min(t, ROW_TILE)
    for t0 in range(0, t, tt):
        sel = (lax.broadcasted_iota(jnp.int32, (tt, MOE_SLOTS), 0) + t0 == idx_ref[0]).astype(_BF)
        part = _dot(sel, ye)

        @pl.when(eg == 0)
        def _():
            o_ref[0, t0:t0 + tt, :] = part

        @pl.when(eg > 0)
        def _():
            o_ref[0, t0:t0 + tt, :] += part


def _moe_scatter(ye, idx, gate, t):
    b, e, cap = idx.shape
    eb = MOE_SLOTS // cap
    return pl.pallas_call(
        _scatter_kernel,
        grid=(b, e // eb),
        in_specs=[pl.BlockSpec((1, 1, MOE_SLOTS), lambda bi, eg: (bi, 0, eg)),
                  pl.BlockSpec((1, MOE_SLOTS, 1), lambda bi, eg: (bi, eg, 0)),
                  pl.BlockSpec((eb, cap, D_MODEL), lambda bi, eg: (eg, bi, 0))],
        out_specs=pl.BlockSpec((1, t, D_MODEL), lambda bi, eg: (bi, 0, 0)),
        out_shape=jax.ShapeDtypeStruct((b, t, D_MODEL), jnp.float32),
        compiler_params=pltpu.CompilerParams(dimension_semantics=("arbitrary", "arbitrary"),
                                             vmem_limit_bytes=VMEM_LIMIT),
        name="moe_scatter",
    )(idx.reshape(b, 1, e * cap), gate.reshape(b, e * cap, 1), ye)


def _final_kernel(x1_ref, moe_ref, mod_ref, g_ref, o_ref):
    x2 = x1_ref[...] + mod_ref[0][5:6] * moe_ref[...]
    o_ref[...] = x2 * lax.rsqrt(jnp.mean(x2 * x2, axis=-1, keepdims=True) + RMS_EPS) * g_ref[...]


def _final(x1, moe, mod, mod_base, rows_per_mod, final_g):
    n = x1.shape[0]
    tm = ROW_TILE
    mod_idx = lambda i: (mod_base + (i * tm) // rows_per_mod, 0, 0)
    row = lambda i: (i, 0)
    return pl.pallas_call(
        _final_kernel,
        grid=(n // tm,),
        in_specs=[pl.BlockSpec((tm, D_MODEL), row), pl.BlockSpec((tm, D_MODEL), row),
                  pl.BlockSpec((1, 6, D_MODEL), mod_idx),
                  pl.BlockSpec((1, D_MODEL), lambda i: (0, 0))],
        out_specs=pl.BlockSpec((tm, D_MODEL), row),
        out_shape=jax.ShapeDtypeStruct((n, D_MODEL), jnp.float32),
        compiler_params=pltpu.CompilerParams(dimension_semantics=("arbitrary",),
                                             vmem_limit_bytes=VMEM_LIMIT),
        name="final_norm",
    )(x1, moe, mod, final_g.reshape(1, D_MODEL))


def _centred_shift(p, mu_prev, mu_next):
    zero = jnp.zeros_like(p[:, :1])
    p_prev = jnp.concatenate([zero, p[:, :-1]], axis=1)
    p_next = jnp.concatenate([p[:, 1:], zero], axis=1)
    return p + mu_prev * (p_prev - p) + mu_next * (p_next - p)


def _route(logits, b, t):
    cap = EC_FACTOR * t // N_EXPERTS
    aff = jax.nn.softmax(logits.reshape(b, t, N_EXPERTS), axis=-1)
    gate, idx = lax.top_k(jnp.swapaxes(aff, 1, 2), cap)
    return gate, idx


def kernel(x_prompt, x_sample, cache_na_k, cache_na_v, state_rwkv, c, c_ctx, final_norm_g, norm1_g, norm2_g,
           w_mod, b_mod, w_in, mu_prev, mu_next, w0, w_up, a0, a_up, g_up, k_k, k_a, r_k, gn_g, gn_b, rpb,
           w_out, router, e_gate, e_up, e_down):
    bp, tp, _ = x_prompt.shape
    bs, ts, _ = x_sample.shape
    l = 0
    cond8 = jnp.concatenate([c_ctx[None, :], c, jnp.zeros((8 - 1 - bs, D_MODEL), jnp.float32)], axis=0)
    mod = _adaln(cond8, w_mod[l], b_mod[l]).reshape(8, 6, D_MODEL)
    w_in_bf = w_in[l].astype(_BF)
    w_out_bf = w_out[l].astype(_BF)
    router_pad = jnp.pad(router[l], ((0, 0), (0, 128 - N_EXPERTS)))

    groups = []
    for name, x, mod_base, s0 in (
            ("prompt", x_prompt, 0, jnp.zeros((bp, 2, N_A, D_A), jnp.float32)),
            ("sample", x_sample, 1,
             jnp.transpose(state_rwkv[:, l], (0, 1, 3, 2, 4)).reshape(bs, 2, N_A, D_A))):
        b, t, _ = x.shape
        is_prompt = name == "prompt"
        rows_per_mod = b * t if is_prompt else t
        x2d = x.reshape(b * t, D_MODEL)
        pa, qkv = _inproj(x2d, mod, mod_base, rows_per_mod, norm1_g[l], w_in_bf,
                          jnp.float32 if is_prompt else _BF)
        pa = _centred_shift(pa.reshape(b, t, D_A_COLS), mu_prev[l], mu_next[l])
        qkv = qkv.reshape(b, t, 3 * D_B)
        y, z, s_fin = _rwkv_scan(pa, s0, w0[l], w_up[l], a0[l], a_up[l], k_k[l], k_a[l], r_k[l].reshape(D_A))
        if is_prompt:
            yb = _context_attention(qkv)
        else:
            yb = _neighbourhood_attention(qkv, cache_na_k[:, l], cache_na_v[:, l], rpb[l])
        x1, h2, logits = _outproj(y.reshape(2, b * t, D_A), z.reshape(2, b * t, D_A),
                                  pa.reshape(b * t, D_A_COLS), yb.reshape(b * t, D_B), x2d, mod, mod_base,
                                  rows_per_mod, norm2_g[l], w_out_bf, router_pad, g_up[l], gn_g[l], gn_b[l])
        gate, idx = _route(logits[:, :N_EXPERTS], b, t)
        groups.append(dict(b=b, t=t, x1=x1, h2=h2, gate=gate, idx=idx, qkv=qkv, s_fin=s_fin,
                           mod_base=mod_base, rows_per_mod=rows_per_mod))

    xes = [_moe_gather(gr["h2"].reshape(gr["b"], gr["t"], D_MODEL), gr["idx"]) for gr in groups]
    yes = _moe_experts(xes[0], xes[1], e_gate[l], e_up[l], e_down[l])

    outs = []
    for gr, ye in zip(groups, yes):
        b, t = gr["b"], gr["t"]
        moe = _moe_scatter(ye, gr["idx"], gr["gate"], t)
        y = _final(gr["x1"], moe.reshape(b * t, D_MODEL), mod, gr["mod_base"], gr["rows_per_mod"], final_norm_g)
        outs.append(y.reshape(b, t, D_MODEL))

    gp = groups[0]
    new_k = gp["qkv"][:, :, D_B:2 * D_B].reshape(bp, 1, tp, H_B, N_B)
    new_v = gp["qkv"][:, :, 2 * D_B:].reshape(bp, 1, tp, H_B, N_B)
    new_s = jnp.transpose(gp["s_fin"].reshape(bp, 2, N_A, H_A, N_A), (0, 1, 3, 2, 4))
    return (outs[0], outs[1], new_k, new_v, new_s[:, None])
```
